```python
import math
import jax, jax.numpy as jnp
from jax import lax
import numpy as np

D_MODEL = 4096
BATCH = 4
SEQ = 4096
DEPTH = 1

CHUNK = 64
N_MEM = 256
SSM_WIDTH = D_MODEL // 2
CONV_WIDTH = D_MODEL - SSM_WIDTH
MIX_WIDTH = SSM_WIDTH + CONV_WIDTH
SSM_GROUP = 16
SSM_GROUPS = SSM_WIDTH // SSM_GROUP
SSM_STATE = 64
SHORT_CONV = 3
XATTN_HEADS = 4
XATTN_HEAD_DIM = D_MODEL // XATTN_HEADS
D_FF = ((8 * D_MODEL // 3 + 255) // 256) * 256
FFN_CONV = 3
EPS = 1e-6
DT_MIN = 1e-3
DT_MAX = 1e-1
PROJ_IN_WIDTH = SSM_WIDTH + 3 * CONV_WIDTH

kernel_name = "hymba_s5_shortconv_convffn_memxattn"


def rmsnorm(x, g):
    xf = x.astype(jnp.float32)
    y = xf * lax.rsqrt(jnp.mean(xf * xf, axis=-1, keepdims=True) + EPS)
    return (y * g.astype(jnp.float32)).astype(x.dtype)


def causal_dwconv(x, w):
    k_width = w.shape[0]
    length = x.shape[1]
    xp = jnp.pad(x, ((0, 0), (k_width - 1, 0), (0, 0)))
    y = xp[:, 0:length] * w[0]
    for k in range(1, k_width):
        y = y + xp[:, k:k + length] * w[k]
    return y


def _cmul(ar, ai, br, bi):
    return ar * br - ai * bi, ar * bi + ai * br


def _scan_combine(e1, e2):
    a1r, a1i, b1r, b1i = e1
    a2r, a2i, b2r, b2i = e2
    ar, ai = _cmul(a2r, a2i, a1r, a1i)
    br, bi = _cmul(a2r, a2i, b1r, b1i)
    return ar, ai, br + b2r, bi + b2i


def s5_mixer(u, lam_re, lam_im, log_step, b_re, b_im, c_re, c_im, d, glu_w, glu_b):
    f32 = jnp.float32
    bsz, length, _ = u.shape
    G, H, P, T = SSM_GROUPS, SSM_GROUP, SSM_STATE, CHUNK
    uf = u.astype(f32).reshape(bsz, length, G, H)
    lr = lam_re.astype(f32)
    li = lam_im.astype(f32)
    step = jnp.exp(log_step.astype(f32))[:, None]
    mag = jnp.exp(lr * step)
    ang = li * step
    abar_re, abar_im = mag * jnp.cos(ang), mag * jnp.sin(ang)
    den = lr * lr + li * li
    nr, ni = abar_re - 1.0, abar_im
    coef_re = (nr * lr + ni * li) / den
    coef_im = (ni * lr - nr * li) / den
    br_f, bi_f = b_re.astype(f32), b_im.astype(f32)
    bbar_re = coef_re[..., None] * br_f - coef_im[..., None] * bi_f
    bbar_im = coef_re[..., None] * bi_f + coef_im[..., None] * br_f
    cr_f, ci_f = c_re.astype(f32), c_im.astype(f32)
    kk = jnp.arange(1, T + 1, dtype=f32)[:, None, None]
    pmag = jnp.exp(lr * step * kk)
    pang = li * step * kk
    pow_re = (pmag * jnp.cos(pang))[:, None]
    pow_im = (pmag * jnp.sin(pang))[:, None]
    a_re = jnp.broadcast_to(abar_re, (T, 1, G, P))
    a_im = jnp.broadcast_to(abar_im, (T, 1, G, P))
    n_chunks = length // T
    uc = uf.reshape(bsz, n_chunks, T, G, H).transpose(1, 2, 0, 3, 4)

    def chunk_step(carry, u_c):
        sr, si = carry
        bu_re = jnp.einsum('tbgh,gph->tbgp', u_c, bbar_re)
        bu_im = jnp.einsum('tbgh,gph->tbgp', u_c, bbar_im)
        _, _, hr, hi = lax.associative_scan(_scan_combine, (a_re, a_im, bu_re, bu_im), axis=0)
        cr, ci = _cmul(pow_re, pow_im, sr[None], si[None])
        hr = hr + cr
        hi = hi + ci
        y = jnp.einsum('tbgp,ghp->tbgh', hr, cr_f) - jnp.einsum('tbgp,ghp->tbgh', hi, ci_f)
        return (hr[-1], hi[-1]), y

    init = (jnp.zeros((bsz, G, P), f32), jnp.zeros((bsz, G, P), f32))
    _, ys = lax.scan(chunk_step, init, uc)
    y = ys.transpose(2, 0, 1, 3, 4).reshape(bsz, length, G, H)
    y = y + d.astype(f32).reshape(G, H) * uf
    z = jax.nn.gelu(y.reshape(bsz, length, SSM_WIDTH)).astype(u.dtype)
    return z * jax.nn.sigmoid(z @ glu_w + glu_b)


def setup_inputs(seed: int = 0) -> dict:
    key = jax.random.key(seed)
    ks = jax.random.split(key, 32)
    f32 = jnp.float32
    nrm = lambda k, shape, scale: jax.random.normal(k, shape, f32) * scale
    gain = lambda k, n: 1.0 + 0.02 * jax.random.normal(k, (DEPTH, n), f32)
    G, H, P = SSM_GROUPS, SSM_GROUP, SSM_STATE
    lam_im_base = math.pi * jnp.arange(P, dtype=f32)
    return {
        "x": jax.random.normal(ks[0], (BATCH, SEQ, D_MODEL), f32),
        "mem": jax.random.normal(ks[1], (BATCH, N_MEM, D_MODEL), f32),
        "norm_mix_g": gain(ks[2], D_MODEL),
        "w_in": nrm(ks[3], (DEPTH, D_MODEL, PROJ_IN_WIDTH), D_MODEL ** -0.5),
        "ssm_lambda_re": -0.5 + 0.01 * jax.random.normal(ks[4], (DEPTH, G, P), f32),
        "ssm_lambda_im": lam_im_base + 0.01 * jax.random.normal(ks[5], (DEPTH, G, P), f32),
        "ssm_log_step": jax.random.uniform(ks[6], (DEPTH, G), f32, math.log(DT_MIN), math.log(DT_MAX)),
        "ssm_b_re": nrm(ks[7], (DEPTH, G, P, H), (2 * H) ** -0.5),
        "ssm_b_im": nrm(ks[8], (DEPTH, G, P, H), (2 * H) ** -0.5),
        "ssm_c_re": nrm(ks[9], (DEPTH, G, H, P), P ** -0.5),
        "ssm_c_im": nrm(ks[10], (DEPTH, G, H, P), P ** -0.5),
        "ssm_d": nrm(ks[11], (DEPTH, SSM_WIDTH), 1.0),
        "ssm_glu_w": nrm(ks[12], (DEPTH, SSM_WIDTH, SSM_WIDTH), SSM_WIDTH ** -0.5),
        "ssm_glu_b": nrm(ks[13], (DEPTH, SSM_WIDTH), 0.01),
        "conv_w": nrm(ks[14], (DEPTH, SHORT_CONV, CONV_WIDTH), SHORT_CONV ** -0.5),
        "out_norm_ssm_g": gain(ks[15], SSM_WIDTH),
        "out_norm_conv_g": gain(ks[16], CONV_WIDTH),
        "w_out": nrm(ks[17], (DEPTH, MIX_WIDTH, D_MODEL), MIX_WIDTH ** -0.5),
        "norm_xattn_g": gain(ks[18], D_MODEL),
        "norm_mem_g": gain(ks[19], D_MODEL),
        "xattn_wq": nrm(ks[20], (DEPTH, D_MODEL, D_MODEL), D_MODEL ** -0.5),
        "xattn_wk": nrm(ks[21], (DEPTH, D_MODEL, D_MODEL), D_MODEL ** -0.5),
        "xattn_wv": nrm(ks[22], (DEPTH, D_MODEL, D_MODEL), D_MODEL ** -0.5),
        "xattn_wo": nrm(ks[23], (DEPTH, D_MODEL, D_MODEL), D_MODEL ** -0.5),
        "norm_ffn_g": gain(ks[24], D_MODEL),
        "ffn_w_up": nrm(ks[25], (DEPTH, D_MODEL, 2 * D_FF), D_MODEL ** -0.5),
        "ffn_conv_w": nrm(ks[26], (DEPTH, FFN_CONV, D_FF), FFN_CONV ** -0.5),
        "ffn_conv_b": nrm(ks[27], (DEPTH, D_FF), 0.01),
        "ffn_w_down": nrm(ks[28], (DEPTH, D_FF, D_MODEL), D_FF ** -0.5),
        "norm_final_g": 1.0 + 0.02 * jax.random.normal(ks[29], (D_MODEL,), f32),
    }


def reference(x, mem, norm_mix_g, w_in, ssm_lambda_re, ssm_lambda_im, ssm_log_step,
              ssm_b_re, ssm_b_im, ssm_c_re, ssm_c_im, ssm_d, ssm_glu_w, ssm_glu_b,
              conv_w, out_norm_ssm_g, out_norm_conv_g, w_out,
              norm_xattn_g, norm_mem_g, xattn_wq, xattn_wk, xattn_wv, xattn_wo,
              norm_ffn_g, ffn_w_up, ffn_conv_w, ffn_conv_b, ffn_w_down, norm_final_g):
    bsz, length, _ = x.shape
    n_mem = mem.shape[1]
    split_pts = [SSM_WIDTH, SSM_WIDTH + CONV_WIDTH, SSM_WIDTH + 2 * CONV_WIDTH]
    for l in range(DEPTH):
        h = rmsnorm(x, norm_mix_g[l])
        proj = h @ w_in[l]
        u_ssm, gate_b, gate_c, v = jnp.split(proj, split_pts, axis=-1)
        y_ssm = s5_mixer(u_ssm, ssm_lambda_re[l], ssm_lambda_im[l], ssm_log_step[l],
                         ssm_b_re[l], ssm_b_im[l], ssm_c_re[l], ssm_c_im[l], ssm_d[l],
                         ssm_glu_w[l], ssm_glu_b[l])
        y_conv = gate_b * causal_dwconv(gate_c * v, conv_w[l])
        mixed = jnp.concatenate([rmsnorm(y_ssm, out_norm_ssm_g[l]),
                                 rmsnorm(y_conv, out_norm_conv_g[l])], axis=-1)
        x = x + mixed @ w_out[l]
        hq = rmsnorm(x, norm_xattn_g[l])
        hm = rmsnorm(mem, norm_mem_g[l])
        q = (hq @ xattn_wq[l]).reshape(bsz, length, XATTN_HEADS, XATTN_HEAD_DIM)
        k = (hm @ xattn_wk[l]).reshape(bsz, n_mem, XATTN_HEADS, XATTN_HEAD_DIM)
        vv = (hm @ xattn_wv[l]).reshape(bsz, n_mem, XATTN_HEADS, XATTN_HEAD_DIM)
        s = jnp.einsum('bqhd,bkhd->bhqk', q, k).astype(jnp.float32) * (XATTN_HEAD_DIM ** -0.5)
        p = jax.nn.softmax(s, axis=-1).astype(x.dtype)
        o = jnp.einsum('bhqk,bkhd->bqhd', p, vv).reshape(bsz, length, D_MODEL)
        x = x + o @ xattn_wo[l]
        h = rmsnorm(x, norm_ffn_g[l])
        up = h @ ffn_w_up[l]
        a, g = jnp.split(up, [D_FF], axis=-1)
        a = causal_dwconv(a, ffn_conv_w[l]) + ffn_conv_b[l]
        x = x + (jax.nn.silu(a) * g) @ ffn_w_down[l]
    return rmsnorm(x, norm_final_g)
```

```python
import functools

import jax
import jax.numpy as jnp
from jax import lax
from jax.experimental import pallas as pl
from jax.experimental.pallas import tpu as pltpu

F32 = jnp.float32
BF16 = jnp.bfloat16

EPS = 1e-6
SSM_GROUP = 16
SSM_STATE = 64
XATTN_HEADS = 4
V7X_VMEM_LIMIT_BYTES = 56 * 1024 * 1024

LANES = 128
SUBLANES = 8
GROUPS_PER_BLOCK = LANES // SSM_GROUP
STATE_LANES = GROUPS_PER_BLOCK * SSM_STATE
SCAN_SUB = 32
SCAN_UNIT = SUBLANES * SCAN_SUB


def _cparams(*sem):
    return pltpu.CompilerParams(dimension_semantics=sem,
                                vmem_limit_bytes=V7X_VMEM_LIMIT_BYTES)


def _rms_scale(xf):
    return lax.rsqrt(jnp.mean(xf * xf, axis=-1, keepdims=True) + EPS)


def _rmsnorm_kernel(x_ref, g_ref, o_ref):
    xf = x_ref[...].astype(F32)
    o_ref[...] = (xf * _rms_scale(xf) * g_ref[...]).astype(o_ref.dtype)


def _rmsnorm(x, g, out_dtype, tm):
    m, d = x.shape
    return pl.pallas_call(
        _rmsnorm_kernel,
        grid=(m // tm,),
        in_specs=[pl.BlockSpec((tm, d), lambda i: (i, 0)),
                  pl.BlockSpec((1, d), lambda i: (0, 0))],
        out_specs=pl.BlockSpec((tm, d), lambda i: (i, 0)),
        out_shape=jax.ShapeDtypeStruct((m, d), out_dtype),
        compiler_params=_cparams("arbitrary"),
        name="rmsnorm",
    )(x, g.reshape(1, d).astype(F32))


def _mm_kernel(a_ref, b_ref, o_ref):
    o_ref[...] = jnp.dot(a_ref[...], b_ref[...],
                         preferred_element_type=F32).astype(o_ref.dtype)


def _mm_res_kernel(a_ref, b_ref, r_ref, o_ref):
    acc = jnp.dot(a_ref[...], b_ref[...], preferred_element_type=F32)
    o_ref[...] = (acc + r_ref[...]).astype(o_ref.dtype)


def _matmul(a, b, *, tm, tn, out_dtype, residual=None, name="matmul"):
    m, k = a.shape
    _, n = b.shape
    in_specs = [pl.BlockSpec((tm, k), lambda i, j: (i, 0)),
                pl.BlockSpec((k, tn), lambda i, j: (0, j))]
    args = [a, b]
    kern = _mm_kernel
    if residual is not None:
        in_specs.append(pl.BlockSpec((tm, tn), lambda i, j: (i, j)))
        args.append(residual)
        kern = _mm_res_kernel
    return pl.pallas_call(
        kern,
        grid=(m // tm, n // tn),
        in_specs=in_specs,
        out_specs=pl.BlockSpec((tm, tn), lambda i, j: (i, j)),
        out_shape=jax.ShapeDtypeStruct((m, n), out_dtype),
        compiler_params=_cparams("arbitrary", "arbitrary"),
        name=name,
    )(*args)


def _cmul(ar, ai, br, bi):
    return ar * br - ai * bi, ar * bi + ai * br


def _shift_rows_down(x, k):
    rows = lax.broadcasted_iota(jnp.int32, x.shape, 0)
    return jnp.where(rows >= k, pltpu.roll(x, k, 0), 0.0)


def _ssm_kernel(u_ref, wb_ref, wc_ref, cp_ref, d_ref, z_ref, s_ref, carry_ref,
                *, units_per_step, steps_per_seq):
    n_chunks = STATE_LANES // LANES

    @pl.when(pl.program_id(1) % steps_per_seq == 0)
    def _():
        carry_ref[...] = jnp.zeros_like(carry_ref)

    def rows_of(i):
        return pl.ds(i, SUBLANES, stride=SCAN_SUB)

    def const(k, c):
        rows = pl.ds(k * SUBLANES, SUBLANES)
        return (cp_ref[0, rows, pl.ds(c * LANES, LANES)],
                cp_ref[0, rows, pl.ds(STATE_LANES + c * LANES, LANES)])

    base = SCAN_SUB

    def scan_chunk(c):
        s_re = s_ref.at[c]
        s_im = s_ref.at[n_chunks + c]
        a_re, a_im = const(0, c)
        h_re = jnp.zeros((SUBLANES, LANES), F32)
        h_im = jnp.zeros((SUBLANES, LANES), F32)
        for i in range(SCAN_SUB):
            p_re, p_im = _cmul(a_re, a_im, h_re, h_im)
            h_re = p_re + s_re[rows_of(i), :]
            h_im = p_im + s_im[rows_of(i), :]
            s_re[rows_of(i), :] = h_re
            s_im[rows_of(i), :] = h_im

        f_re, f_im = h_re, h_im
        for lvl, k in enumerate((1, 2, 4)):
            m_re, m_im = const(base + 1 + lvl, c)
            q_re, q_im = _cmul(m_re, m_im, _shift_rows_down(f_re, k), _shift_rows_down(f_im, k))
            f_re = f_re + q_re
            f_im = f_im + q_im

        lanes_re = pl.ds(c * LANES, LANES)
        lanes_im = pl.ds(STATE_LANES + c * LANES, LANES)
        c_re = carry_ref[:, lanes_re]
        c_im = carry_ref[:, lanes_im]
        in_re, in_im = _cmul(*const(base, c), c_re, c_im)
        in_re = in_re + _shift_rows_down(f_re, 1)
        in_im = in_im + _shift_rows_down(f_im, 1)

        n_re, n_im = _cmul(*const(base + 4, c), c_re, c_im)
        last = SUBLANES - 1
        carry_ref[:, lanes_re] = n_re + jnp.broadcast_to(f_re[last:last + 1], (SUBLANES, LANES))
        carry_ref[:, lanes_im] = n_im + jnp.broadcast_to(f_im[last:last + 1], (SUBLANES, LANES))

        for i in range(SCAN_SUB):
            p_re, p_im = _cmul(*const(i, c), in_re, in_im)
            s_re[rows_of(i), :] = s_re[rows_of(i), :] + p_re
            s_im[rows_of(i), :] = s_im[rows_of(i), :] + p_im

    def unit(un, _):
        r0 = pl.multiple_of(un * SCAN_UNIT, SCAN_UNIT)
        u = u_ref[pl.ds(r0, SCAN_UNIT), :]
        bu = jnp.dot(u, wb_ref[0], preferred_element_type=F32)
        for c in range(2 * n_chunks):
            s_ref[c] = bu[:, c * LANES:(c + 1) * LANES]
        for c in range(n_chunks):
            scan_chunk(c)
        hs = jnp.concatenate([s_ref[c].astype(BF16) for c in range(2 * n_chunks)], axis=1)
        y = jnp.dot(hs, wc_ref[0], preferred_element_type=F32)
        y = y + d_ref[0] * u.astype(F32)
        z_ref[pl.ds(r0, SCAN_UNIT), :] = jax.nn.gelu(y).astype(z_ref.dtype)
        return 0

    lax.fori_loop(0, units_per_step, unit, 0)


def _ssm(proj, wb, wc, cp, d, *, seq_len, tt):
    m = proj.shape[0]
    nb = wb.shape[0]
    kern = functools.partial(_ssm_kernel, units_per_step=tt // SCAN_UNIT,
                             steps_per_seq=seq_len // tt)
    return pl.pallas_call(
        kern,
        grid=(nb, m // tt),
        in_specs=[pl.BlockSpec((tt, LANES), lambda b, t: (t, b)),
                  pl.BlockSpec((1, LANES, 2 * STATE_LANES), lambda b, t: (b, 0, 0)),
                  pl.BlockSpec((1, 2 * STATE_LANES, LANES), lambda b, t: (b, 0, 0)),
                  pl.BlockSpec((1,) + cp.shape[1:], lambda b, t: (b, 0, 0)),
                  pl.BlockSpec((1, 1, LANES), lambda b, t: (b, 0, 0))],
        out_specs=pl.BlockSpec((tt, LANES), lambda b, t: (t, b)),
        out_shape=jax.ShapeDtypeStruct((m, nb * LANES), BF16),
        scratch_shapes=[pltpu.VMEM((2 * STATE_LANES // LANES, SCAN_UNIT, LANES), F32),
                        pltpu.VMEM((SUBLANES, 2 * STATE_LANES), F32)],
        compiler_params=_cparams("arbitrary", "arbitrary"),
        name="s5_scan",
    )(proj, wb, wc, cp, d)


def _ssm_params(lam_re, lam_im, log_step, b_re, b_im, c_re, c_im, d):
    g, p = lam_re.shape
    h = b_re.shape[-1]
    nb = g // GROUPS_PER_BLOCK
    lr = lam_re.astype(F32)
    li = lam_im.astype(F32)
    step = jnp.exp(log_step.astype(F32))[:, None]
    mag = jnp.exp(lr * step)
    ang = li * step
    abar_re, abar_im = mag * jnp.cos(ang), mag * jnp.sin(ang)
    den = lr * lr + li * li
    nr, ni = abar_re - 1.0, abar_im
    coef_re = (nr * lr + ni * li) / den
    coef_im = (ni * lr - nr * li) / den
    br_f, bi_f = b_re.astype(F32), b_im.astype(F32)
    bbar_re = coef_re[..., None] * br_f - coef_im[..., None] * bi_f
    bbar_im = coef_re[..., None] * bi_f + coef_im[..., None] * br_f
    eye = jnp.eye(GROUPS_PER_BLOCK, dtype=F32)

    def in_block(x):
        x = x.reshape(nb, GROUPS_PER_BLOCK, p, h).transpose(0, 1, 3, 2)
        x = jnp.einsum('lghp,gk->lghkp', x, eye)
        return x.reshape(nb, LANES, STATE_LANES)

    def out_block(x):
        x = x.reshape(nb, GROUPS_PER_BLOCK, h, p).transpose(0, 1, 3, 2)
        x = jnp.einsum('lgph,gk->lgpkh', x, eye)
        return x.reshape(nb, STATE_LANES, LANES)

    wb = jnp.concatenate([in_block(bbar_re), in_block(bbar_im)], axis=-1).astype(BF16)
    wc = jnp.concatenate([out_block(c_re.astype(F32)), out_block(-c_im.astype(F32))],
                         axis=1).astype(BF16)

    a_re = abar_re.reshape(nb, STATE_LANES)
    a_im = abar_im.reshape(nb, STATE_LANES)
    pows = [(a_re, a_im)]
    for _ in range(SCAN_SUB - 1):
        pows.append(_cmul(a_re, a_im, *pows[-1]))
    sub = pows[-1]
    one = (jnp.ones_like(a_re), jnp.zeros_like(a_im))
    per_sublane = [one]
    for _ in range(SUBLANES - 1):
        per_sublane.append(_cmul(*sub, *per_sublane[-1]))
    sub2 = _cmul(*sub, *sub)
    sub4 = _cmul(*sub2, *sub2)
    sub8 = _cmul(*sub4, *sub4)

    def rep(c):
        row = jnp.concatenate(c, axis=-1)[:, None, :]
        return jnp.broadcast_to(row, (nb, SUBLANES, 2 * STATE_LANES))

    sub_rows = jnp.stack([jnp.concatenate(c, axis=-1) for c in per_sublane], axis=1)
    cp = jnp.concatenate([rep(c) for c in pows] + [sub_rows]
                         + [rep(c) for c in (sub, sub2, sub4, sub8)], axis=1)
    return wb, wc, cp, d.astype(F32).reshape(nb, 1, LANES)


def _mix_kernel(z_ref, gw_ref, gb_ref, ng_s_ref, pb_ref, pc_ref, pv_ref, hc_ref, hv_ref,
                cw_ref, ng_c_ref, o_ref, w_ref, *, tm, width, tiles_per_seq):
    z = z_ref[...]
    lin = jnp.dot(z, gw_ref[...], preferred_element_type=F32) + gb_ref[...]
    y = z.astype(F32) * jax.nn.sigmoid(lin)
    o_ref[:, pl.ds(0, width)] = (y * _rms_scale(y) * ng_s_ref[...]).astype(o_ref.dtype)

    cv = pc_ref[...].astype(F32) * pv_ref[...].astype(F32)
    halo = hc_ref[...].astype(F32) * hv_ref[...].astype(F32)
    first = pl.program_id(0) % tiles_per_seq == 0
    w_ref[pl.ds(0, SUBLANES), :] = jnp.where(first, 0.0, halo)
    w_ref[pl.ds(SUBLANES, tm), :] = cv
    conv = (cw_ref[0:1, :] * w_ref[pl.ds(SUBLANES - 2, tm), :]
            + cw_ref[1:2, :] * w_ref[pl.ds(SUBLANES - 1, tm), :]
            + cw_ref[2:3, :] * cv)
    yc = pb_ref[...].astype(F32) * conv
    o_ref[:, pl.ds(width, width)] = (yc * _rms_scale(yc) * ng_c_ref[...]).astype(o_ref.dtype)


def _mix(z, proj, glu_w, glu_b, ng_s, conv_w, ng_c, *, seq_len, tm):
    m, width = z.shape
    cb = width // width
    halo_blocks = tm // SUBLANES
    kern = functools.partial(_mix_kernel, tm=tm, width=width, tiles_per_seq=seq_len // tm)
    row = lambda a: a.reshape(1, width).astype(F32)

    def halo_map(col):
        return lambda i: (jnp.maximum(i * halo_blocks - 1, 0), col)

    return pl.pallas_call(
        kern,
        grid=(m // tm,),
        in_specs=[pl.BlockSpec((tm, width), lambda i: (i, 0)),
                  pl.BlockSpec((width, width), lambda i: (0, 0)),
                  pl.BlockSpec((1, width), lambda i: (0, 0)),
                  pl.BlockSpec((1, width), lambda i: (0, 0)),
                  pl.BlockSpec((tm, width), lambda i: (i, 1 * cb)),
                  pl.BlockSpec((tm, width), lambda i: (i, 2 * cb)),
                  pl.BlockSpec((tm, width), lambda i: (i, 3 * cb)),
                  pl.BlockSpec((SUBLANES, width), halo_map(2 * cb)),
                  pl.BlockSpec((SUBLANES, width), halo_map(3 * cb)),
                  pl.BlockSpec((3, width), lambda i: (0, 0)),
                  pl.BlockSpec((1, width), lambda i: (0, 0))],
        out_specs=pl.BlockSpec((tm, 2 * width), lambda i: (i, 0)),
        out_shape=jax.ShapeDtypeStruct((m, 2 * width), BF16),
        scratch_shapes=[pltpu.VMEM((tm + SUBLANES, width), F32)],
        compiler_params=_cparams("arbitrary"),
        name="mixer_tail",
    )(z, glu_w, row(glu_b), row(ng_s), proj, proj, proj, proj, proj,
      conv_w.astype(F32), row(ng_c))


def _xattn_kernel(hq_ref, wq_ref, k_ref, v_ref, o_ref, *, scale):
    q = jnp.dot(hq_ref[...], wq_ref[...], preferred_element_type=F32)
    s = lax.dot_general(q.astype(BF16), k_ref[...], (((1,), (1,)), ((), ())),
                        preferred_element_type=F32) * scale
    s = s - jnp.max(s, axis=-1, keepdims=True)
    e = jnp.exp(s)
    p = e / jnp.sum(e, axis=-1, keepdims=True)
    o_ref[...] = jnp.dot(p.astype(BF16), v_ref[...],
                         preferred_element_type=F32).astype(o_ref.dtype)


def _xattn(hq, wq, k, v, *, seq_len, n_mem, tm):
    m, d = hq.shape
    hd = d // XATTN_HEADS
    tiles_per_seq = seq_len // tm
    kern = functools.partial(_xattn_kernel, scale=float(hd) ** -0.5)
    return pl.pallas_call(
        kern,
        grid=(m // tm, XATTN_HEADS),
        in_specs=[pl.BlockSpec((tm, d), lambda i, h: (i, 0)),
                  pl.BlockSpec((d, hd), lambda i, h: (0, h)),
                  pl.BlockSpec((n_mem, hd), lambda i, h: (i // tiles_per_seq, h)),
                  pl.BlockSpec((n_mem, hd), lambda i, h: (i // tiles_per_seq, h))],
        out_specs=pl.BlockSpec((tm, hd), lambda i, h: (i, h)),
        out_shape=jax.ShapeDtypeStruct((m, d), BF16),
        compiler_params=_cparams("arbitrary", "arbitrary"),
        name="xattn",
    )(hq, wq, k, v)


def _ffn_up_kernel(h_ref, wa_ref, wg_ref, cw_ref, cb_ref, o_ref, w_ref, halo_ref,
                   *, tm, tiles_per_seq):
    j = pl.program_id(1)
    a = jnp.dot(h_ref[...], wa_ref[...], preferred_element_type=F32)
    g = jnp.dot(h_ref[...], wg_ref[...], preferred_element_type=F32)
    first = pl.program_id(0) % tiles_per_seq == 0
    w_ref[pl.ds(0, SUBLANES), :] = jnp.where(first, 0.0, halo_ref[j])
    w_ref[pl.ds(SUBLANES, tm), :] = a
    halo_ref[j] = a[tm - SUBLANES:, :]
    conv = (cw_ref[0:1, :] * w_ref[pl.ds(SUBLANES - 2, tm), :]
            + cw_ref[1:2, :] * w_ref[pl.ds(SUBLANES - 1, tm), :]
            + cw_ref[2:3, :] * a) + cb_ref[...]
    o_ref[...] = (jax.nn.silu(conv) * g).astype(o_ref.dtype)


def _ffn_up(h, w_up, conv_w, conv_b, *, seq_len, tm, tf):
    m, d = h.shape
    d_ff = conv_w.shape[-1]
    nf = d_ff // tf
    kern = functools.partial(_ffn_up_kernel, tm=tm, tiles_per_seq=seq_len // tm)
    return pl.pallas_call(
        kern,
        grid=(m // tm, nf),
        in_specs=[pl.BlockSpec((tm, d), lambda i, j: (i, 0)),
                  pl.BlockSpec((d, tf), lambda i, j: (0, j)),
                  pl.BlockSpec((d, tf), lambda i, j: (0, j + nf)),
                  pl.BlockSpec((3, tf), lambda i, j: (0, j)),
                  pl.BlockSpec((1, tf), lambda i, j: (0, j))],
        out_specs=pl.BlockSpec((tm, tf), lambda i, j: (i, j)),
        out_shape=jax.ShapeDtypeStruct((m, d_ff), BF16),
        scratch_shapes=[pltpu.VMEM((tm + SUBLANES, tf), F32),
                        pltpu.VMEM((nf, SUBLANES, tf), F32)],
        compiler_params=_cparams("arbitrary", "arbitrary"),
        name="ffn_up",
    )(h, w_up, w_up, conv_w.astype(F32), conv_b.reshape(1, d_ff).astype(F32))


def kernel(x, mem, norm_mix_g, w_in, ssm_lambda_re, ssm_lambda_im, ssm_log_step, ssm_b_re, ssm_b_im, ssm_c_re, ssm_c_im, ssm_d, ssm_glu_w, ssm_glu_b, conv_w, out_norm_ssm_g, out_norm_conv_g, w_out, norm_xattn_g, norm_mem_g, xattn_wq, xattn_wk, xattn_wv, xattn_wo, norm_ffn_g, ffn_w_up, ffn_conv_w, ffn_conv_b, ffn_w_down, norm_final_g):
    bsz, seq_len, d_model = x.shape
    n_mem = mem.shape[1]
    depth = w_in.shape[0]
    m = bsz * seq_len
    xs = x.reshape(m, d_model)
    mems = mem.reshape(bsz * n_mem, d_model)
    for l in range(depth):
        h = _rmsnorm(xs, norm_mix_g[l], BF16, tm=512)
        proj = _matmul(h, w_in[l].astype(BF16), tm=1024, tn=1024, out_dtype=BF16, name="proj_in")
        wb, wc, cp, d = _ssm_params(ssm_lambda_re[l], ssm_lambda_im[l], ssm_log_step[l],
                                    ssm_b_re[l], ssm_b_im[l], ssm_c_re[l], ssm_c_im[l], ssm_d[l])
        z = _ssm(proj, wb, wc, cp, d, seq_len=seq_len, tt=1024)
        mixed = _mix(z, proj, ssm_glu_w[l].astype(BF16), ssm_glu_b[l], out_norm_ssm_g[l],
                     conv_w[l], out_norm_conv_g[l], seq_len=seq_len, tm=256)
        xs = _matmul(mixed, w_out[l].astype(BF16), tm=1024, tn=512, out_dtype=F32,
                     residual=xs, name="proj_out")
        hq = _rmsnorm(xs, norm_xattn_g[l], BF16, tm=512)
        hm = _rmsnorm(mems, norm_mem_g[l], BF16, tm=512)
        k = _matmul(hm, xattn_wk[l].astype(BF16), tm=1024, tn=1024, out_dtype=BF16, name="proj_k")
        v = _matmul(hm, xattn_wv[l].astype(BF16), tm=1024, tn=1024, out_dtype=BF16, name="proj_v")
        o = _xattn(hq, xattn_wq[l].astype(BF16), k, v, seq_len=seq_len, n_mem=n_mem, tm=1024)
        xs = _matmul(o, xattn_wo[l].astype(BF16), tm=1024, tn=512, out_dtype=F32,
                     residual=xs, name="proj_o")
        h = _rmsnorm(xs, norm_ffn_g[l], BF16, tm=512)
        act = _ffn_up(h, ffn_w_up[l].astype(BF16), ffn_conv_w[l], ffn_conv_b[l],
                      seq_len=seq_len, tm=1024, tf=256)
        xs = _matmul(act, ffn_w_down[l].astype(BF16), tm=512, tn=512, out_dtype=F32,
                     residual=xs, name="ffn_down")
    out = _rmsnorm(xs, norm_final_g, x.dtype, tm=512)
    return out.reshape(bsz, seq_len, d_model)
```

```python
import functools

import jax
import jax.numpy as jnp
from jax import lax
from jax.experimental import pallas as pl
from jax.experimental.pallas import tpu as pltpu

F32 = jnp.float32
BF16 = jnp.bfloat16

EPS = 1e-6
SSM_GROUP = 16
SSM_STATE = 64
XATTN_HEADS = 4
V7X_VMEM_LIMIT_BYTES = 56 * 1024 * 1024

LANES = 128
SUBLANES = 8
GROUPS_PER_BLOCK = LANES // SSM_GROUP
STATE_LANES = GROUPS_PER_BLOCK * SSM_STATE
SCAN_SUB = 32
SCAN_UNIT = SUBLANES * SCAN_SUB


def _cparams(*sem, flags=None):
    return pltpu.CompilerParams(dimension_semantics=sem, flags=flags,
                                vmem_limit_bytes=V7X_VMEM_LIMIT_BYTES)


def _rms_scale(xf):
    return lax.rsqrt(jnp.mean(xf * xf, axis=-1, keepdims=True) + EPS)


def _rmsnorm_kernel(x_ref, g_ref, o_ref):
    xf = x_ref[...].astype(F32)
    o_ref[...] = (xf * _rms_scale(xf) * g_ref[...]).astype(o_ref.dtype)


def _rmsnorm(x, g, out_dtype, tm):
    m, d = x.shape
    return pl.pallas_call(
        _rmsnorm_kernel,
        grid=(m // tm,),
        in_specs=[pl.BlockSpec((tm, d), lambda i: (i, 0)),
                  pl.BlockSpec((1, d), lambda i: (0, 0))],
        out_specs=pl.BlockSpec((tm, d), lambda i: (i, 0)),
        out_shape=jax.ShapeDtypeStruct((m, d), out_dtype),
        compiler_params=_cparams("arbitrary"),
        name="rmsnorm",
    )(x, g.reshape(1, d).astype(F32))


def _mm_kernel(a_ref, b_ref, o_ref):
    o_ref[...] = jnp.dot(a_ref[...], b_ref[...],
                         preferred_element_type=F32).astype(o_ref.dtype)


def _mm_res_kernel(a_ref, b_ref, r_ref, o_ref):
    acc = jnp.dot(a_ref[...], b_ref[...], preferred_element_type=F32)
    o_ref[...] = (acc + r_ref[...]).astype(o_ref.dtype)


def _matmul(a, b, *, tm, tn, out_dtype, residual=None, name="matmul"):
    m, k = a.shape
    _, n = b.shape
    in_specs = [pl.BlockSpec((tm, k), lambda i, j: (i, 0)),
                pl.BlockSpec((k, tn), lambda i, j: (0, j))]
    args = [a, b]
    kern = _mm_kernel
    if residual is not None:
        in_specs.append(pl.BlockSpec((tm, tn), lambda i, j: (i, j)))
        args.append(residual)
        kern = _mm_res_kernel
    return pl.pallas_call(
        kern,
        grid=(m // tm, n // tn),
        in_specs=in_specs,
        out_specs=pl.BlockSpec((tm, tn), lambda i, j: (i, j)),
        out_shape=jax.ShapeDtypeStruct((m, n), out_dtype),
        compiler_params=_cparams("arbitrary", "arbitrary"),
        name=name,
    )(*args)


def _cmul(ar, ai, br, bi):
    return ar * br - ai * bi, ar * bi + ai * br


def _shift_rows_down(x, k):
    rows = lax.broadcasted_iota(jnp.int32, x.shape, 0)
    return jnp.where(rows >= k, pltpu.roll(x, k, 0), 0.0)


def _ssm_kernel(u_ref, perm_ref, unperm_ref, wb_ref, wc_ref, cp_ref, d_ref, z_ref,
                s_all_ref, carry_ref, *, units_per_step, steps_per_seq):
    sl = STATE_LANES
    re = pl.ds(0, sl)
    im = pl.ds(sl, sl)

    @pl.when(pl.program_id(1) % steps_per_seq == 0)
    def _():
        carry_ref[...] = jnp.zeros_like(carry_ref)

    def rows_of(i):
        return pl.ds(i * SUBLANES, SUBLANES)

    def const(k):
        rows = pl.ds(k * SUBLANES, SUBLANES)
        return cp_ref[0, rows, re], cp_ref[0, rows, im]

    base = SCAN_SUB

    def unit(un):
        r0 = un * SCAN_UNIT
        s_ref = s_all_ref.at[un]
        u = jnp.dot(perm_ref[...], u_ref[pl.ds(r0, SCAN_UNIT), :],
                    preferred_element_type=F32).astype(BF16)
        s_ref[...] = jnp.dot(u, wb_ref[0], preferred_element_type=F32)

        a_re, a_im = const(0)
        h_re = jnp.zeros((SUBLANES, sl), F32)
        h_im = jnp.zeros((SUBLANES, sl), F32)
        for i in range(SCAN_SUB):
            p_re, p_im = _cmul(a_re, a_im, h_re, h_im)
            h_re = p_re + s_ref[rows_of(i), re]
            h_im = p_im + s_ref[rows_of(i), im]
            s_ref[rows_of(i), re] = h_re
            s_ref[rows_of(i), im] = h_im

        f_re, f_im = h_re, h_im
        for lvl, k in enumerate((1, 2, 4)):
            q_re, q_im = _cmul(*const(base + 1 + lvl),
                               _shift_rows_down(f_re, k), _shift_rows_down(f_im, k))
            f_re = f_re + q_re
            f_im = f_im + q_im

        c_re = carry_ref[:, re]
        c_im = carry_ref[:, im]
        in_re, in_im = _cmul(*const(base), c_re, c_im)
        in_re = in_re + _shift_rows_down(f_re, 1)
        in_im = in_im + _shift_rows_down(f_im, 1)

        n_re, n_im = _cmul(*const(base + 4), c_re, c_im)
        last = SUBLANES - 1
        carry_ref[:, re] = n_re + jnp.broadcast_to(f_re[last:last + 1], (SUBLANES, sl))
        carry_ref[:, im] = n_im + jnp.broadcast_to(f_im[last:last + 1], (SUBLANES, sl))

        for i in range(SCAN_SUB):
            p_re, p_im = _cmul(*const(i), in_re, in_im)
            s_ref[rows_of(i), re] = s_ref[rows_of(i), re] + p_re
            s_ref[rows_of(i), im] = s_ref[rows_of(i), im] + p_im

        y = jnp.dot(s_ref[...].astype(BF16), wc_ref[0], preferred_element_type=F32)
        y = y + d_ref[0] * u.astype(F32)
        z = jax.nn.gelu(y).astype(BF16)
        z_ref[pl.ds(r0, SCAN_UNIT), :] = jnp.dot(
            unperm_ref[...], z, preferred_element_type=F32).astype(z_ref.dtype)

    for un in range(units_per_step):
        unit(un)


def _ssm(proj, wb, wc, cp, d, *, seq_len, tt):
    m = proj.shape[0]
    nb = wb.shape[0]
    kern = functools.partial(_ssm_kernel, units_per_step=tt // SCAN_UNIT,
                             steps_per_seq=seq_len // tt)
    r = jnp.arange(SCAN_UNIT)
    src = (r % SUBLANES) * SCAN_SUB + r // SUBLANES
    perm = (src[:, None] == r[None, :]).astype(BF16)
    return pl.pallas_call(
        kern,
        grid=(nb, m // tt),
        in_specs=[pl.BlockSpec((tt, LANES), lambda b, t: (t, b)),
                  pl.BlockSpec((SCAN_UNIT, SCAN_UNIT), lambda b, t: (0, 0)),
                  pl.BlockSpec((SCAN_UNIT, SCAN_UNIT), lambda b, t: (0, 0)),
                  pl.BlockSpec((1, LANES, 2 * STATE_LANES), lambda b, t: (b, 0, 0)),
                  pl.BlockSpec((1, 2 * STATE_LANES, LANES), lambda b, t: (b, 0, 0)),
                  pl.BlockSpec((1,) + cp.shape[1:], lambda b, t: (b, 0, 0)),
                  pl.BlockSpec((1, 1, LANES), lambda b, t: (b, 0, 0))],
        out_specs=pl.BlockSpec((tt, LANES), lambda b, t: (t, b)),
        out_shape=jax.ShapeDtypeStruct((m, nb * LANES), BF16),
        scratch_shapes=[pltpu.VMEM((tt // SCAN_UNIT, SCAN_UNIT, 2 * STATE_LANES), F32),
                        pltpu.VMEM((SUBLANES, 2 * STATE_LANES), F32)],
        compiler_params=_cparams("arbitrary", "arbitrary"),
        name="s5_scan",
    )(proj, perm, perm.T, wb, wc, cp, d)


def _ssm_params(lam_re, lam_im, log_step, b_re, b_im, c_re, c_im, d):
    g, p = lam_re.shape
    h = b_re.shape[-1]
    nb = g // GROUPS_PER_BLOCK
    lr = lam_re.astype(F32)
    li = lam_im.astype(F32)
    step = jnp.exp(log_step.astype(F32))[:, None]
    mag = jnp.exp(lr * step)
    ang = li * step
    abar_re, abar_im = mag * jnp.cos(ang), mag * jnp.sin(ang)
    den = lr * lr + li * li
    nr, ni = abar_re - 1.0, abar_im
    coef_re = (nr * lr + ni * li) / den
    coef_im = (ni * lr - nr * li) / den
    br_f, bi_f = b_re.astype(F32), b_im.astype(F32)
    bbar_re = coef_re[..., None] * br_f - coef_im[..., None] * bi_f
    bbar_im = coef_re[..., None] * bi_f + coef_im[..., None] * br_f
    eye = jnp.eye(GROUPS_PER_BLOCK, dtype=F32)

    def in_block(x):
        x = x.reshape(nb, GROUPS_PER_BLOCK, p, h).transpose(0, 1, 3, 2)
        x = jnp.einsum('lghp,gk->lghkp', x, eye)
        return x.reshape(nb, LANES, STATE_LANES)

    def out_block(x):
        x = x.reshape(nb, GROUPS_PER_BLOCK, h, p).transpose(0, 1, 3, 2)
        x = jnp.einsum('lgph,gk->lgpkh', x, eye)
        return x.reshape(nb, STATE_LANES, LANES)

    wb = jnp.concatenate([in_block(bbar_re), in_block(bbar_im)], axis=-1).astype(BF16)
    wc = jnp.concatenate([out_block(c_re.astype(F32)), out_block(-c_im.astype(F32))],
                         axis=1).astype(BF16)

    a_re = abar_re.reshape(nb, STATE_LANES)
    a_im = abar_im.reshape(nb, STATE_LANES)
    pows = [(a_re, a_im)]
    for _ in range(SCAN_SUB - 1):
        pows.append(_cmul(a_re, a_im, *pows[-1]))
    sub = pows[-1]
    one = (jnp.ones_like(a_re), jnp.zeros_like(a_im))
    per_sublane = [one]
    for _ in range(SUBLANES - 1):
        per_sublane.append(_cmul(*sub, *per_sublane[-1]))
    sub2 = _cmul(*sub, *sub)
    sub4 = _cmul(*sub2, *sub2)
    sub8 = _cmul(*sub4, *sub4)

    def rep(c):
        row = jnp.concatenate(c, axis=-1)[:, None, :]
        return jnp.broadcast_to(row, (nb, SUBLANES, 2 * STATE_LANES))

    sub_rows = jnp.stack([jnp.concatenate(c, axis=-1) for c in per_sublane], axis=1)
    cp = jnp.concatenate([rep(c) for c in pows] + [sub_rows]
                         + [rep(c) for c in (sub, sub2, sub4, sub8)], axis=1)
    return wb, wc, cp, d.astype(F32).reshape(nb, 1, LANES)


def _mix_kernel(z_ref, gw_ref, gb_ref, ng_s_ref, pb_ref, pc_ref, pv_ref, hc_ref, hv_ref,
                cw_ref, ng_c_ref, o_ref, w_ref, *, tm, width, tiles_per_seq):
    z = z_ref[...]
    lin = jnp.dot(z, gw_ref[...], preferred_element_type=F32) + gb_ref[...]
    y = z.astype(F32) * jax.nn.sigmoid(lin)
    o_ref[:, pl.ds(0, width)] = (y * _rms_scale(y) * ng_s_ref[...]).astype(o_ref.dtype)

    cv = pc_ref[...].astype(F32) * pv_ref[...].astype(F32)
    halo = hc_ref[...].astype(F32) * hv_ref[...].astype(F32)
    first = pl.program_id(0) % tiles_per_seq == 0
    w_ref[pl.ds(0, SUBLANES), :] = jnp.where(first, 0.0, halo)
    w_ref[pl.ds(SUBLANES, tm), :] = cv
    conv = (cw_ref[0:1, :] * w_ref[pl.ds(SUBLANES - 2, tm), :]
            + cw_ref[1:2, :] * w_ref[pl.ds(SUBLANES - 1, tm), :]
            + cw_ref[2:3, :] * cv)
    yc = pb_ref[...].astype(F32) * conv
    o_ref[:, pl.ds(width, width)] = (yc * _rms_scale(yc) * ng_c_ref[...]).astype(o_ref.dtype)


def _mix(z, proj, glu_w, glu_b, ng_s, conv_w, ng_c, *, seq_len, tm):
    m, width = z.shape
    cb = width // width
    halo_blocks = tm // SUBLANES
    kern = functools.partial(_mix_kernel, tm=tm, width=width, tiles_per_seq=seq_len // tm)
    row = lambda a: a.reshape(1, width).astype(F32)

    def halo_map(col):
        return lambda i: (jnp.maximum(i * halo_blocks - 1, 0), col)

    return pl.pallas_call(
        kern,
        grid=(m // tm,),
        in_specs=[pl.BlockSpec((tm, width), lambda i: (i, 0)),
                  pl.BlockSpec((width, width), lambda i: (0, 0)),
                  pl.BlockSpec((1, width), lambda i: (0, 0)),
                  pl.BlockSpec((1, width), lambda i: (0, 0)),
                  pl.BlockSpec((tm, width), lambda i: (i, 1 * cb)),
                  pl.BlockSpec((tm, width), lambda i: (i, 2 * cb)),
                  pl.BlockSpec((tm, width), lambda i: (i, 3 * cb)),
                  pl.BlockSpec((SUBLANES, width), halo_map(2 * cb)),
                  pl.BlockSpec((SUBLANES, width), halo_map(3 * cb)),
                  pl.BlockSpec((3, width), lambda i: (0, 0)),
                  pl.BlockSpec((1, width), lambda i: (0, 0))],
        out_specs=pl.BlockSpec((tm, 2 * width), lambda i: (i, 0)),
        out_shape=jax.ShapeDtypeStruct((m, 2 * width), BF16),
        scratch_shapes=[pltpu.VMEM((tm + SUBLANES, width), F32)],
        compiler_params=_cparams("arbitrary"),
        name="mixer_tail",
    )(z, glu_w, row(glu_b), row(ng_s), proj, proj, proj, proj, proj,
      conv_w.astype(F32), row(ng_c))


def _xattn_kernel(hq_ref, wq_ref, k_ref, v_ref, o_ref, *, scale):
    q = jnp.dot(hq_ref[...], wq_ref[...], preferred_element_type=F32)
    s = lax.dot_general(q.astype(BF16), k_ref[...], (((1,), (1,)), ((), ())),
                        preferred_element_type=F32) * scale
    s = s - jnp.max(s, axis=-1, keepdims=True)
    e = jnp.exp(s)
    p = e / jnp.sum(e, axis=-1, keepdims=True)
    o_ref[...] = jnp.dot(p.astype(BF16), v_ref[...],
                         preferred_element_type=F32).astype(o_ref.dtype)


def _xattn(hq, wq, k, v, *, seq_len, n_mem, tm):
    m, d = hq.shape
    hd = d // XATTN_HEADS
    tiles_per_seq = seq_len // tm
    kern = functools.partial(_xattn_kernel, scale=float(hd) ** -0.5)
    return pl.pallas_call(
        kern,
        grid=(m // tm, XATTN_HEADS),
        in_specs=[pl.BlockSpec((tm, d), lambda i, h: (i, 0)),
                  pl.BlockSpec((d, hd), lambda i, h: (0, h)),
                  pl.BlockSpec((n_mem, hd), lambda i, h: (i // tiles_per_seq, h)),
                  pl.BlockSpec((n_mem, hd), lambda i, h: (i // tiles_per_seq, h))],
        out_specs=pl.BlockSpec((tm, hd), lambda i, h: (i, h)),
        out_shape=jax.ShapeDtypeStruct((m, d), BF16),
        compiler_params=_cparams("arbitrary", "arbitrary"),
        name="xattn",
    )(hq, wq, k, v)


def _ffn_up_kernel(h_ref, wa_ref, wg_ref, cw_ref, cb_ref, o_ref,
                   wab_ref, wgb_ref, a0_ref, g0_ref, a1_ref, g1_ref, stage_ref,
                   *, tm, n_i, n_steps, tiles_per_seq, n_phase):
    t = pl.program_id(0)
    i = t % n_i

    @pl.when(jnp.logical_and(i == 0, t < n_steps - 1))
    def _():
        wab_ref[...] = wa_ref[...].astype(BF16)
        wgb_ref[...] = wg_ref[...].astype(BF16)

    mc = tm // n_phase
    rc = 32

    def tail_rows(a_prev, g_prev, r0):
        conv = (cw_ref[0:1, :] * a_prev[pl.ds(SUBLANES - 2 + r0, rc), :]
                + cw_ref[1:2, :] * a_prev[pl.ds(SUBLANES - 1 + r0, rc), :]
                + cw_ref[2:3, :] * a_prev[pl.ds(SUBLANES + r0, rc), :]) + cb_ref[...]
        stage_ref[pl.ds(r0, rc), :] = (jax.nn.silu(conv)
                                       * g_prev[pl.ds(r0, rc), :]).astype(stage_ref.dtype)

    def step(cur, prev, dots, tail):
        kc = h_ref.shape[1] // n_phase
        a_acc = g_acc = None
        for p in range(n_phase):
            if dots:
                hb = h_ref[:, pl.ds(p * kc, kc)]
                pa = jnp.dot(hb, wab_ref[pl.ds(p * kc, kc), :], preferred_element_type=F32)
                pg = jnp.dot(hb, wgb_ref[pl.ds(p * kc, kc), :], preferred_element_type=F32)
                a_acc = pa if a_acc is None else a_acc + pa
                g_acc = pg if g_acc is None else g_acc + pg
            if tail:
                for r0 in range(p * mc, (p + 1) * mc, rc):
                    tail_rows(prev[0], prev[1], r0)
        if tail:
            o_ref[...] = stage_ref[...]
        if dots:
            a_cur, g_cur = cur
            a_cur[pl.ds(SUBLANES, tm), :] = a_acc
            g_cur[...] = g_acc
            if prev is None:
                a_cur[pl.ds(0, SUBLANES), :] = jnp.zeros((SUBLANES, a_cur.shape[1]), F32)
            else:
                seq_start = i % tiles_per_seq == 0
                a_cur[pl.ds(0, SUBLANES), :] = jnp.where(seq_start, 0.0,
                                                         prev[0][pl.ds(tm, SUBLANES), :])

    slots = ((a0_ref, g0_ref), (a1_ref, g1_ref))
    middle = jnp.logical_and(t > 0, t < n_steps - 1)

    @pl.when(t == 0)
    def _():
        step(slots[0], None, dots=True, tail=False)

    for parity in (0, 1):
        @pl.when(jnp.logical_and(middle, t % 2 == parity))
        def _():
            step(slots[parity], slots[1 - parity], dots=True, tail=True)

    @pl.when(t == n_steps - 1)
    def _():
        step(None, slots[(n_steps - 2) % 2], dots=False, tail=True)


def _ffn_up(h, w_up, conv_w, conv_b, *, seq_len, tm, tf):
    m, d = h.shape
    d_ff = conv_w.shape[-1]
    nf = d_ff // tf
    n_i = m // tm
    n_steps = nf * n_i + 1
    kern = functools.partial(_ffn_up_kernel, tm=tm, n_i=n_i, n_steps=n_steps, n_phase=4,
                             tiles_per_seq=seq_len // tm)

    def tile_j(t):
        return jnp.minimum(t, n_steps - 2) // n_i

    def tail_i(t):
        return jnp.maximum(t - 1, 0) % n_i

    def tail_j(t):
        return jnp.maximum(t - 1, 0) // n_i

    return pl.pallas_call(
        kern,
        grid=(n_steps,),
        in_specs=[pl.BlockSpec((tm, d), lambda t: (jnp.minimum(t, n_steps - 2) % n_i, 0)),
                  pl.BlockSpec((d, tf), lambda t: (0, tile_j(t))),
                  pl.BlockSpec((d, tf), lambda t: (0, tile_j(t) + nf)),
                  pl.BlockSpec((3, tf), lambda t: (0, tail_j(t))),
                  pl.BlockSpec((1, tf), lambda t: (0, tail_j(t)))],
        out_specs=pl.BlockSpec((tm, tf), lambda t: (tail_i(t), tail_j(t))),
        out_shape=jax.ShapeDtypeStruct((m, d_ff), BF16),
        scratch_shapes=[pltpu.VMEM((d, tf), BF16), pltpu.VMEM((d, tf), BF16),
                        pltpu.VMEM((tm + SUBLANES, tf), F32), pltpu.VMEM((tm, tf), F32),
                        pltpu.VMEM((tm + SUBLANES, tf), F32), pltpu.VMEM((tm, tf), F32),
                        pltpu.VMEM((tm, tf), BF16)],
        compiler_params=_cparams("arbitrary"),
        name="ffn_up",
    )(h, w_up, w_up, conv_w.astype(F32), conv_b.reshape(1, d_ff).astype(F32))


def kernel(x, mem, norm_mix_g, w_in, ssm_lambda_re, ssm_lambda_im, ssm_log_step, ssm_b_re, ssm_b_im, ssm_c_re, ssm_c_im, ssm_d, ssm_glu_w, ssm_glu_b, conv_w, out_norm_ssm_g, out_norm_conv_g, w_out, norm_xattn_g, norm_mem_g, xattn_wq, xattn_wk, xattn_wv, xattn_wo, norm_ffn_g, ffn_w_up, ffn_conv_w, ffn_conv_b, ffn_w_down, norm_final_g):
    bsz, seq_len, d_model = x.shape
    n_mem = mem.shape[1]
    depth = w_in.shape[0]
    m = bsz * seq_len
    xs = x.reshape(m, d_model)
    mems = mem.reshape(bsz * n_mem, d_model)
    for l in range(depth):
        h = _rmsnorm(xs, norm_mix_g[l], BF16, tm=512)
        proj = _matmul(h, w_in[l].astype(BF16), tm=1024, tn=1024, out_dtype=BF16, name="proj_in")
        wb, wc, cp, d = _ssm_params(ssm_lambda_re[l], ssm_lambda_im[l], ssm_log_step[l],
                                    ssm_b_re[l], ssm_b_im[l], ssm_c_re[l], ssm_c_im[l], ssm_d[l])
        z = _ssm(proj, wb, wc, cp, d, seq_len=seq_len, tt=1024)
        mixed = _mix(z, proj, ssm_glu_w[l].astype(BF16), ssm_glu_b[l], out_norm_ssm_g[l],
                     conv_w[l], out_norm_conv_g[l], seq_len=seq_len, tm=256)
        xs = _matmul(mixed, w_out[l].astype(BF16), tm=1024, tn=512, out_dtype=F32,
                     residual=xs, name="proj_out")
        hq = _rmsnorm(xs, norm_xattn_g[l], BF16, tm=512)
        hm = _rmsnorm(mems, norm_mem_g[l], BF16, tm=512)
        k = _matmul(hm, xattn_wk[l].astype(BF16), tm=1024, tn=1024, out_dtype=BF16, name="proj_k")
        v = _matmul(hm, xattn_wv[l].astype(BF16), tm=1024, tn=1024, out_dtype=BF16, name="proj_v")
        o = _xattn(hq, xattn_wq[l].astype(BF16), k, v, seq_len=seq_len, n_mem=n_mem, tm=1024)
        xs = _matmul(o, xattn_wo[l].astype(BF16), tm=1024, tn=512, out_dtype=F32,
                     residual=xs, name="proj_o")
        h = _rmsnorm(xs, norm_ffn_g[l], BF16, tm=512)
        act = _ffn_up(h, ffn_w_up[l], ffn_conv_w[l], ffn_conv_b[l],
                      seq_len=seq_len, tm=1024, tf=256)
        xs = _matmul(act, ffn_w_down[l].astype(BF16), tm=512, tn=512, out_dtype=F32,
                     residual=xs, name="ffn_down")
    out = _rmsnorm(xs, norm_final_g, x.dtype, tm=512)
    return out.reshape(bsz, seq_len, d_model)
```

```python
import functools

import jax
import jax.numpy as jnp
from jax import lax
from jax.experimental import pallas as pl
from jax.experimental.pallas import tpu as pltpu

F32 = jnp.float32
BF16 = jnp.bfloat16

EPS = 1e-6
SSM_GROUP = 16
SSM_STATE = 64
XATTN_HEADS = 4
V7X_VMEM_LIMIT_BYTES = 56 * 1024 * 1024

LANES = 128
SUBLANES = 8
GROUPS_PER_BLOCK = LANES // SSM_GROUP
STATE_LANES = GROUPS_PER_BLOCK * SSM_STATE
SCAN_SUB = 32
SCAN_UNIT = SUBLANES * SCAN_SUB


def _cparams(*sem, flags=None):
    return pltpu.CompilerParams(dimension_semantics=sem, flags=flags,
                                vmem_limit_bytes=V7X_VMEM_LIMIT_BYTES)


def _rms_scale(xf):
    return lax.rsqrt(jnp.mean(xf * xf, axis=-1, keepdims=True) + EPS)


def _rmsnorm_kernel(x_ref, g_ref, o_ref):
    xf = x_ref[...].astype(F32)
    o_ref[...] = (xf * _rms_scale(xf) * g_ref[...]).astype(o_ref.dtype)


def _rmsnorm(x, g, out_dtype, tm):
    m, d = x.shape
    return pl.pallas_call(
        _rmsnorm_kernel,
        grid=(m // tm,),
        in_specs=[pl.BlockSpec((tm, d), lambda i: (i, 0)),
                  pl.BlockSpec((1, d), lambda i: (0, 0))],
        out_specs=pl.BlockSpec((tm, d), lambda i: (i, 0)),
        out_shape=jax.ShapeDtypeStruct((m, d), out_dtype),
        compiler_params=_cparams("arbitrary"),
        name="rmsnorm",
    )(x, g.reshape(1, d).astype(F32))


def _mm_kernel(a_ref, b_ref, o_ref):
    o_ref[...] = jnp.dot(a_ref[...], b_ref[...],
                         preferred_element_type=F32).astype(o_ref.dtype)


def _mm_res_kernel(a_ref, b_ref, r_ref, o_ref):
    acc = jnp.dot(a_ref[...], b_ref[...], preferred_element_type=F32)
    o_ref[...] = (acc + r_ref[...]).astype(o_ref.dtype)


def _matmul(a, b, *, tm, tn, out_dtype, residual=None, name="matmul"):
    m, k = a.shape
    _, n = b.shape
    in_specs = [pl.BlockSpec((tm, k), lambda i, j: (i, 0)),
                pl.BlockSpec((k, tn), lambda i, j: (0, j))]
    args = [a, b]
    kern = _mm_kernel
    if residual is not None:
        in_specs.append(pl.BlockSpec((tm, tn), lambda i, j: (i, j)))
        args.append(residual)
        kern = _mm_res_kernel
    return pl.pallas_call(
        kern,
        grid=(m // tm, n // tn),
        in_specs=in_specs,
        out_specs=pl.BlockSpec((tm, tn), lambda i, j: (i, j)),
        out_shape=jax.ShapeDtypeStruct((m, n), out_dtype),
        compiler_params=_cparams("arbitrary", "arbitrary"),
        name=name,
    )(*args)


def _mm_res_norm_kernel(a_ref, b_ref, r_ref, g_ref, o_ref, xg_ref, ssq_ref):
    x = jnp.dot(a_ref[...], b_ref[...], preferred_element_type=F32) + r_ref[...]
    o_ref[...] = x
    xg_ref[...] = (x * g_ref[...]).astype(xg_ref.dtype)
    ssq_ref[0] = jnp.sum(x * x, axis=-1, keepdims=True)


def _matmul_res_norm(a, b, residual, gain, *, tm, tn, name):
    m, k = a.shape
    _, n = b.shape
    nj = n // tn
    return pl.pallas_call(
        _mm_res_norm_kernel,
        grid=(m // tm, nj),
        in_specs=[pl.BlockSpec((tm, k), lambda i, j: (i, 0)),
                  pl.BlockSpec((k, tn), lambda i, j: (0, j)),
                  pl.BlockSpec((tm, tn), lambda i, j: (i, j)),
                  pl.BlockSpec((1, tn), lambda i, j: (0, j))],
        out_specs=[pl.BlockSpec((tm, tn), lambda i, j: (i, j)),
                   pl.BlockSpec((tm, tn), lambda i, j: (i, j)),
                   pl.BlockSpec((1, tm, 1), lambda i, j: (j, i, 0))],
        out_shape=[jax.ShapeDtypeStruct((m, n), F32),
                   jax.ShapeDtypeStruct((m, n), BF16),
                   jax.ShapeDtypeStruct((nj, m, 1), F32)],
        compiler_params=_cparams("arbitrary", "arbitrary"),
        name=name,
    )(a, b, residual, gain.reshape(1, n).astype(F32))


def _row_scale_kernel(ssq_ref, o_ref, *, width):
    ms = jnp.sum(ssq_ref[...], axis=0) * (1.0 / width)
    o_ref[...] = jnp.broadcast_to(lax.rsqrt(ms + EPS), o_ref.shape)


def _row_scale(ssq, width, *, tm):
    nj, m, _ = ssq.shape
    return pl.pallas_call(
        functools.partial(_row_scale_kernel, width=width),
        grid=(m // tm,),
        in_specs=[pl.BlockSpec((nj, tm, 1), lambda i: (0, i, 0))],
        out_specs=pl.BlockSpec((tm, LANES), lambda i: (i, 0)),
        out_shape=jax.ShapeDtypeStruct((m, LANES), F32),
        compiler_params=_cparams("arbitrary"),
        name="row_scale",
    )(ssq)


def _cmul(ar, ai, br, bi):
    return ar * br - ai * bi, ar * bi + ai * br


def _shift_rows_down(x, k):
    rows = lax.broadcasted_iota(jnp.int32, x.shape, 0)
    return jnp.where(rows >= k, pltpu.roll(x, k, 0), 0.0)


def _ssm_kernel(u_ref, perm_ref, unperm_ref, wb_ref, wc_ref, cp_ref, d_ref, z_ref,
                s_all_ref, carry_ref, *, units_per_step, steps_per_seq):
    sl = STATE_LANES
    re = pl.ds(0, sl)
    im = pl.ds(sl, sl)

    @pl.when(pl.program_id(1) % steps_per_seq == 0)
    def _():
        carry_ref[...] = jnp.zeros_like(carry_ref)

    def rows_of(i):
        return pl.ds(i * SUBLANES, SUBLANES)

    def const(k):
        rows = pl.ds(k * SUBLANES, SUBLANES)
        return cp_ref[0, rows, re], cp_ref[0, rows, im]

    base = SCAN_SUB

    units = range(units_per_step)
    s_refs = [s_all_ref.at[un] for un in units]

    us = []
    for un in units:
        u = jnp.dot(perm_ref[...], u_ref[pl.ds(un * SCAN_UNIT, SCAN_UNIT), :],
                    preferred_element_type=F32).astype(BF16)
        s_refs[un][...] = jnp.dot(u, wb_ref[0], preferred_element_type=F32)
        us.append(u)

    a_re, a_im = const(0)
    hs = [(jnp.zeros((SUBLANES, sl), F32), jnp.zeros((SUBLANES, sl), F32)) for _ in units]
    for i in range(SCAN_SUB):
        for un in units:
            p_re, p_im = _cmul(a_re, a_im, *hs[un])
            h_re = p_re + s_refs[un][rows_of(i), re]
            h_im = p_im + s_refs[un][rows_of(i), im]
            s_refs[un][rows_of(i), re] = h_re
            s_refs[un][rows_of(i), im] = h_im
            hs[un] = (h_re, h_im)

    c_re = carry_ref[:, re]
    c_im = carry_ref[:, im]
    carries_in = []
    for un in units:
        f_re, f_im = hs[un]
        for lvl, k in enumerate((1, 2, 4)):
            q_re, q_im = _cmul(*const(base + 1 + lvl),
                               _shift_rows_down(f_re, k), _shift_rows_down(f_im, k))
            f_re = f_re + q_re
            f_im = f_im + q_im
        in_re, in_im = _cmul(*const(base), c_re, c_im)
        carries_in.append((in_re + _shift_rows_down(f_re, 1), in_im + _shift_rows_down(f_im, 1)))
        n_re, n_im = _cmul(*const(base + 4), c_re, c_im)
        last = SUBLANES - 1
        c_re = n_re + jnp.broadcast_to(f_re[last:last + 1], (SUBLANES, sl))
        c_im = n_im + jnp.broadcast_to(f_im[last:last + 1], (SUBLANES, sl))
    carry_ref[:, re] = c_re
    carry_ref[:, im] = c_im

    for un in units:
        s_ref = s_refs[un]
        in_re, in_im = carries_in[un]
        for i in range(SCAN_SUB):
            p_re, p_im = _cmul(*const(i), in_re, in_im)
            s_ref[rows_of(i), re] = s_ref[rows_of(i), re] + p_re
            s_ref[rows_of(i), im] = s_ref[rows_of(i), im] + p_im
        y = jnp.dot(s_ref[...].astype(BF16), wc_ref[0], preferred_element_type=F32)
        y = y + d_ref[0] * us[un].astype(F32)
        z = jax.nn.gelu(y).astype(BF16)
        z_ref[pl.ds(un * SCAN_UNIT, SCAN_UNIT), :] = jnp.dot(
            unperm_ref[...], z, preferred_element_type=F32).astype(z_ref.dtype)


def _ssm(proj, wb, wc, cp, d, *, seq_len, tt):
    m = proj.shape[0]
    nb = wb.shape[0]
    kern = functools.partial(_ssm_kernel, units_per_step=tt // SCAN_UNIT,
                             steps_per_seq=seq_len // tt)
    r = jnp.arange(SCAN_UNIT)
    src = (r % SUBLANES) * SCAN_SUB + r // SUBLANES
    perm = (src[:, None] == r[None, :]).astype(BF16)
    return pl.pallas_call(
        kern,
        grid=(nb, m // tt),
        in_specs=[pl.BlockSpec((tt, LANES), lambda b, t: (t, b)),
                  pl.BlockSpec((SCAN_UNIT, SCAN_UNIT), lambda b, t: (0, 0)),
                  pl.BlockSpec((SCAN_UNIT, SCAN_UNIT), lambda b, t: (0, 0)),
                  pl.BlockSpec((1, LANES, 2 * STATE_LANES), lambda b, t: (b, 0, 0)),
                  pl.BlockSpec((1, 2 * STATE_LANES, LANES), lambda b, t: (b, 0, 0)),
                  pl.BlockSpec((1,) + cp.shape[1:], lambda b, t: (b, 0, 0)),
                  pl.BlockSpec((1, 1, LANES), lambda b, t: (b, 0, 0))],
        out_specs=pl.BlockSpec((tt, LANES), lambda b, t: (t, b)),
        out_shape=jax.ShapeDtypeStruct((m, nb * LANES), BF16),
        scratch_shapes=[pltpu.VMEM((tt // SCAN_UNIT, SCAN_UNIT, 2 * STATE_LANES), F32),
                        pltpu.VMEM((SUBLANES, 2 * STATE_LANES), F32)],
        compiler_params=_cparams("arbitrary", "arbitrary"),
        name="s5_scan",
    )(proj, perm, perm.T, wb, wc, cp, d)


def _ssm_params(lam_re, lam_im, log_step, b_re, b_im, c_re, c_im, d):
    g, p = lam_re.shape
    h = b_re.shape[-1]
    nb = g // GROUPS_PER_BLOCK
    lr = lam_re.astype(F32)
    li = lam_im.astype(F32)
    step = jnp.exp(log_step.astype(F32))[:, None]
    mag = jnp.exp(lr * step)
    ang = li * step
    abar_re, abar_im = mag * jnp.cos(ang), mag * jnp.sin(ang)
    den = lr * lr + li * li
    nr, ni = abar_re - 1.0, abar_im
    coef_re = (nr * lr + ni * li) / den
    coef_im = (ni * lr - nr * li) / den
    br_f, bi_f = b_re.astype(F32), b_im.astype(F32)
    bbar_re = coef_re[..., None] * br_f - coef_im[..., None] * bi_f
    bbar_im = coef_re[..., None] * bi_f + coef_im[..., None] * br_f
    eye = jnp.eye(GROUPS_PER_BLOCK, dtype=F32)

    def in_block(x):
        x = x.reshape(nb, GROUPS_PER_BLOCK, p, h).transpose(0, 1, 3, 2)
        x = jnp.einsum('lghp,gk->lghkp', x, eye)
        return x.reshape(nb, LANES, STATE_LANES)

    def out_block(x):
        x = x.reshape(nb, GROUPS_PER_BLOCK, h, p).transpose(0, 1, 3, 2)
        x = jnp.einsum('lgph,gk->lgpkh', x, eye)
        return x.reshape(nb, STATE_LANES, LANES)

    wb = jnp.concatenate([in_block(bbar_re), in_block(bbar_im)], axis=-1).astype(BF16)
    wc = jnp.concatenate([out_block(c_re.astype(F32)), out_block(-c_im.astype(F32))],
                         axis=1).astype(BF16)

    a_re = abar_re.reshape(nb, STATE_LANES)
    a_im = abar_im.reshape(nb, STATE_LANES)
    pows = [(a_re, a_im)]
    for _ in range(SCAN_SUB - 1):
        pows.append(_cmul(a_re, a_im, *pows[-1]))
    sub = pows[-1]
    one = (jnp.ones_like(a_re), jnp.zeros_like(a_im))
    per_sublane = [one]
    for _ in range(SUBLANES - 1):
        per_sublane.append(_cmul(*sub, *per_sublane[-1]))
    sub2 = _cmul(*sub, *sub)
    sub4 = _cmul(*sub2, *sub2)
    sub8 = _cmul(*sub4, *sub4)

    def rep(c):
        row = jnp.concatenate(c, axis=-1)[:, None, :]
        return jnp.broadcast_to(row, (nb, SUBLANES, 2 * STATE_LANES))

    sub_rows = jnp.stack([jnp.concatenate(c, axis=-1) for c in per_sublane], axis=1)
    cp = jnp.concatenate([rep(c) for c in pows] + [sub_rows]
                         + [rep(c) for c in (sub, sub2, sub4, sub8)], axis=1)
    return wb, wc, cp, d.astype(F32).reshape(nb, 1, LANES)


def _mix_kernel(z_ref, gw_ref, gb_ref, ng_s_ref, pb_ref, pc_ref, pv_ref, hc_ref, hv_ref,
                cw_ref, ng_c_ref, o_ref, w_ref, *, tm, width, tiles_per_seq):
    z = z_ref[...]
    lin = jnp.dot(z, gw_ref[...], preferred_element_type=F32) + gb_ref[...]
    y = z.astype(F32) * jax.nn.sigmoid(lin)
    o_ref[:, pl.ds(0, width)] = (y * _rms_scale(y) * ng_s_ref[...]).astype(o_ref.dtype)

    cv = pc_ref[...].astype(F32) * pv_ref[...].astype(F32)
    halo = hc_ref[...].astype(F32) * hv_ref[...].astype(F32)
    first = pl.program_id(0) % tiles_per_seq == 0
    w_ref[pl.ds(0, SUBLANES), :] = jnp.where(first, 0.0, halo)
    w_ref[pl.ds(SUBLANES, tm), :] = cv
    conv = (cw_ref[0:1, :] * w_ref[pl.ds(SUBLANES - 2, tm), :]
            + cw_ref[1:2, :] * w_ref[pl.ds(SUBLANES - 1, tm), :]
            + cw_ref[2:3, :] * cv)
    yc = pb_ref[...].astype(F32) * conv
    o_ref[:, pl.ds(width, width)] = (yc * _rms_scale(yc) * ng_c_ref[...]).astype(o_ref.dtype)


def _mix(z, proj, glu_w, glu_b, ng_s, conv_w, ng_c, *, seq_len, tm):
    m, width = z.shape
    cb = width // width
    halo_blocks = tm // SUBLANES
    kern = functools.partial(_mix_kernel, tm=tm, width=width, tiles_per_seq=seq_len // tm)
    row = lambda a: a.reshape(1, width).astype(F32)

    def halo_map(col):
        return lambda i: (jnp.maximum(i * halo_blocks - 1, 0), col)

    return pl.pallas_call(
        kern,
        grid=(m // tm,),
        in_specs=[pl.BlockSpec((tm, width), lambda i: (i, 0)),
                  pl.BlockSpec((width, width), lambda i: (0, 0)),
                  pl.BlockSpec((1, width), lambda i: (0, 0)),
                  pl.BlockSpec((1, width), lambda i: (0, 0)),
                  pl.BlockSpec((tm, width), lambda i: (i, 1 * cb)),
                  pl.BlockSpec((tm, width), lambda i: (i, 2 * cb)),
                  pl.BlockSpec((tm, width), lambda i: (i, 3 * cb)),
                  pl.BlockSpec((SUBLANES, width), halo_map(2 * cb)),
                  pl.BlockSpec((SUBLANES, width), halo_map(3 * cb)),
                  pl.BlockSpec((3, width), lambda i: (0, 0)),
                  pl.BlockSpec((1, width), lambda i: (0, 0))],
        out_specs=pl.BlockSpec((tm, 2 * width), lambda i: (i, 0)),
        out_shape=jax.ShapeDtypeStruct((m, 2 * width), BF16),
        scratch_shapes=[pltpu.VMEM((tm + SUBLANES, width), F32)],
        compiler_params=_cparams("arbitrary"),
        name="mixer_tail",
    )(z, glu_w, row(glu_b), row(ng_s), proj, proj, proj, proj, proj,
      conv_w.astype(F32), row(ng_c))


def _xattn_kernel(hq_ref, rs_ref, wq_ref, k_ref, v_ref, o_ref, *, scale):
    q = jnp.dot(hq_ref[...], wq_ref[...], preferred_element_type=F32)
    s = lax.dot_general(q.astype(BF16), k_ref[...], (((1,), (1,)), ((), ())),
                        preferred_element_type=F32)
    rs = rs_ref[...] * scale
    s = s * jnp.concatenate([rs] * (s.shape[1] // LANES), axis=1)
    s = s - jnp.max(s, axis=-1, keepdims=True)
    e = jnp.exp(s)
    p = e / jnp.sum(e, axis=-1, keepdims=True)
    o_ref[...] = jnp.dot(p.astype(BF16), v_ref[...],
                         preferred_element_type=F32).astype(o_ref.dtype)


def _xattn(hq, row_scale, wq, k, v, *, seq_len, n_mem, tm):
    m, d = hq.shape
    hd = d // XATTN_HEADS
    tiles_per_seq = seq_len // tm
    kern = functools.partial(_xattn_kernel, scale=float(hd) ** -0.5)
    return pl.pallas_call(
        kern,
        grid=(m // tm, XATTN_HEADS),
        in_specs=[pl.BlockSpec((tm, d), lambda i, h: (i, 0)),
                  pl.BlockSpec((tm, LANES), lambda i, h: (i, 0)),
                  pl.BlockSpec((d, hd), lambda i, h: (0, h)),
                  pl.BlockSpec((n_mem, hd), lambda i, h: (i // tiles_per_seq, h)),
                  pl.BlockSpec((n_mem, hd), lambda i, h: (i // tiles_per_seq, h))],
        out_specs=pl.BlockSpec((tm, hd), lambda i, h: (i, h)),
        out_shape=jax.ShapeDtypeStruct((m, d), BF16),
        compiler_params=_cparams("arbitrary", "arbitrary"),
        name="xattn",
    )(hq, row_scale, wq, k, v)


def _ffn_up_kernel(h_ref, rs_ref, wa_ref, wg_ref, cw_ref, cb_ref, o_ref,
                   wab_ref, wgb_ref, a0_ref, g0_ref, a1_ref, g1_ref, stage_ref, halo_ref,
                   *, tm, n_i, n_steps, tiles_per_seq, n_phase):
    t = pl.program_id(0)
    i = t % n_i

    @pl.when(jnp.logical_and(i == 0, t < n_steps - 1))
    def _():
        wab_ref[...] = wa_ref[...].astype(BF16)
        wgb_ref[...] = wg_ref[...].astype(BF16)

    mc = tm // n_phase
    rc = 16
    n_cols = o_ref.shape[1] // LANES
    tail_seq_start = (jnp.maximum(t - 1, 0) % n_i) % tiles_per_seq == 0

    def tail_piece(a_prev, g_prev, r0, col, above, zero):
        lanes = pl.ds(col * LANES, LANES)
        rows = pl.ds(r0, rc)
        a = a_prev[rows, lanes]
        if zero is not None:
            a = lax.bitcast_convert_type(lax.bitcast_convert_type(a, jnp.uint32) + zero, F32)
        rs = rs_ref[rows, :]
        a = a * rs
        ext = jnp.concatenate([above, a], axis=0)
        conv = (cw_ref[0:1, lanes] * ext[SUBLANES - 2:SUBLANES - 2 + rc]
                + cw_ref[1:2, lanes] * ext[SUBLANES - 1:SUBLANES - 1 + rc]
                + cw_ref[2:3, lanes] * a) + cb_ref[:, lanes]
        out = jax.nn.silu(conv) * (g_prev[rows, lanes] * rs)
        stage_ref[rows, lanes] = out.astype(stage_ref.dtype)
        bits = lax.bitcast_convert_type(out, jnp.uint32)
        return a[rc - SUBLANES:], (bits >> 16) >> 16

    def step(cur, prev, dots, tail):
        kc = h_ref.shape[1] // n_phase
        a_acc = g_acc = None
        zeros = [None] * n_cols
        if tail:
            above = [jnp.where(tail_seq_start, 0.0, halo_ref[col]) for col in range(n_cols)]
        for p in range(n_phase):
            if dots:
                hb = h_ref[:, pl.ds(p * kc, kc)]
                pa = jnp.dot(hb, wab_ref[pl.ds(p * kc, kc), :], preferred_element_type=F32)
                pg = jnp.dot(hb, wgb_ref[pl.ds(p * kc, kc), :], preferred_element_type=F32)
                a_acc = pa if a_acc is None else a_acc + pa
                g_acc = pg if g_acc is None else g_acc + pg
            if tail:
                for r0 in range(p * mc, (p + 1) * mc, rc):
                    for col in range(n_cols):
                        above[col], zeros[col] = tail_piece(prev[0], prev[1], r0, col,
                                                            above[col], zeros[col])
        if tail:
            for col in range(n_cols):
                halo_ref[col] = above[col]
            o_ref[...] = stage_ref[...]
        if dots:
            cur[0][...] = a_acc
            cur[1][...] = g_acc

    slots = ((a0_ref, g0_ref), (a1_ref, g1_ref))
    middle = jnp.logical_and(t > 0, t < n_steps - 1)

    @pl.when(t == 0)
    def _():
        halo_ref[...] = jnp.zeros_like(halo_ref)
        step(slots[0], None, dots=True, tail=False)

    for parity in (0, 1):
        @pl.when(jnp.logical_and(middle, t % 2 == parity))
        def _():
            step(slots[parity], slots[1 - parity], dots=True, tail=True)

    @pl.when(t == n_steps - 1)
    def _():
        step(None, slots[(n_steps - 2) % 2], dots=False, tail=True)


def _ffn_up(h, row_scale, w_up, conv_w, conv_b, *, seq_len, tm, tf):
    m, d = h.shape
    d_ff = conv_w.shape[-1]
    nf = d_ff // tf
    n_i = m // tm
    n_steps = nf * n_i + 1
    kern = functools.partial(_ffn_up_kernel, tm=tm, n_i=n_i, n_steps=n_steps, n_phase=4,
                             tiles_per_seq=seq_len // tm)

    def tile_j(t):
        return jnp.minimum(t, n_steps - 2) // n_i

    def tail_i(t):
        return jnp.maximum(t - 1, 0) % n_i

    def tail_j(t):
        return jnp.maximum(t - 1, 0) // n_i

    return pl.pallas_call(
        kern,
        grid=(n_steps,),
        in_specs=[pl.BlockSpec((tm, d), lambda t: (jnp.minimum(t, n_steps - 2) % n_i, 0)),
                  pl.BlockSpec((tm, LANES), lambda t: (tail_i(t), 0)),
                  pl.BlockSpec((d, tf), lambda t: (0, tile_j(t))),
                  pl.BlockSpec((d, tf), lambda t: (0, tile_j(t) + nf)),
                  pl.BlockSpec((3, tf), lambda t: (0, tail_j(t))),
                  pl.BlockSpec((1, tf), lambda t: (0, tail_j(t)))],
        out_specs=pl.BlockSpec((tm, tf), lambda t: (tail_i(t), tail_j(t))),
        out_shape=jax.ShapeDtypeStruct((m, d_ff), BF16),
        scratch_shapes=[pltpu.VMEM((d, tf), BF16), pltpu.VMEM((d, tf), BF16),
                        pltpu.VMEM((tm, tf), F32), pltpu.VMEM((tm, tf), F32),
                        pltpu.VMEM((tm, tf), F32), pltpu.VMEM((tm, tf), F32),
                        pltpu.VMEM((tm, tf), BF16),
                        pltpu.VMEM((tf // LANES, SUBLANES, LANES), F32)],
        compiler_params=_cparams("arbitrary"),
        name="ffn_up",
    )(h, row_scale, w_up, w_up, conv_w.astype(F32), conv_b.reshape(1, d_ff).astype(F32))


def kernel(x, mem, norm_mix_g, w_in, ssm_lambda_re, ssm_lambda_im, ssm_log_step, ssm_b_re, ssm_b_im, ssm_c_re, ssm_c_im, ssm_d, ssm_glu_w, ssm_glu_b, conv_w, out_norm_ssm_g, out_norm_conv_g, w_out, norm_xattn_g, norm_mem_g, xattn_wq, xattn_wk, xattn_wv, xattn_wo, norm_ffn_g, ffn_w_up, ffn_conv_w, ffn_conv_b, ffn_w_down, norm_final_g):
    bsz, seq_len, d_model = x.shape
    n_mem = mem.shape[1]
    depth = w_in.shape[0]
    m = bsz * seq_len
    xs = x.reshape(m, d_model)
    mems = mem.reshape(bsz * n_mem, d_model)
    for l in range(depth):
        h = _rmsnorm(xs, norm_mix_g[l], BF16, tm=512)
        proj = _matmul(h, w_in[l].astype(BF16), tm=1024, tn=1024, out_dtype=BF16, name="proj_in")
        wb, wc, cp, d = _ssm_params(ssm_lambda_re[l], ssm_lambda_im[l], ssm_log_step[l],
                                    ssm_b_re[l], ssm_b_im[l], ssm_c_re[l], ssm_c_im[l], ssm_d[l])
        z = _ssm(proj, wb, wc, cp, d, seq_len=seq_len, tt=1024)
        mixed = _mix(z, proj, ssm_glu_w[l].astype(BF16), ssm_glu_b[l], out_norm_ssm_g[l],
                     conv_w[l], out_norm_conv_g[l], seq_len=seq_len, tm=256)
        xs, hq, ssq = _matmul_res_norm(mixed, w_out[l].astype(BF16), xs, norm_xattn_g[l],
                                       tm=1024, tn=512, name="proj_out")
        hm = _rmsnorm(mems, norm_mem_g[l], BF16, tm=512)
        k = _matmul(hm, xattn_wk[l].astype(BF16), tm=1024, tn=1024, out_dtype=BF16, name="proj_k")
        v = _matmul(hm, xattn_wv[l].astype(BF16), tm=1024, tn=1024, out_dtype=BF16, name="proj_v")
        o = _xattn(hq, _row_scale(ssq, d_model, tm=1024), xattn_wq[l].astype(BF16), k, v,
                   seq_len=seq_len, n_mem=n_mem, tm=1024)
        xs, h, ssq = _matmul_res_norm(o, xattn_wo[l].astype(BF16), xs, norm_ffn_g[l],
                                      tm=1024, tn=512, name="proj_o")
        act = _ffn_up(h, _row_scale(ssq, d_model, tm=1024), ffn_w_up[l], ffn_conv_w[l],
                      ffn_conv_b[l], seq_len=seq_len, tm=1024, tf=256)
        xs = _matmul(act, ffn_w_down[l].astype(BF16), tm=512, tn=512, out_dtype=F32,
                     residual=xs, name="ffn_down")
    out = _rmsnorm(xs, norm_final_g, x.dtype, tm=512)
    return out.reshape(bsz, seq_len, d_model)
```

```python
import functools

import jax
import jax.numpy as jnp
from jax import lax
from jax.experimental import pallas as pl
from jax.experimental.pallas import tpu as pltpu

F32 = jnp.float32
BF16 = jnp.bfloat16

EPS = 1e-6
SSM_GROUP = 16
SSM_STATE = 64
XATTN_HEADS = 4
V7X_VMEM_LIMIT_BYTES = 56 * 1024 * 1024

LANES = 128
SUBLANES = 8
GROUPS_PER_BLOCK = LANES // SSM_GROUP
STATE_LANES = GROUPS_PER_BLOCK * SSM_STATE
SCAN_SUB = 32
SCAN_UNIT = SUBLANES * SCAN_SUB


def _cparams(*sem, flags=None):
    return pltpu.CompilerParams(dimension_semantics=sem, flags=flags,
                                vmem_limit_bytes=V7X_VMEM_LIMIT_BYTES)


def _rms_scale(xf):
    return lax.rsqrt(jnp.mean(xf * xf, axis=-1, keepdims=True) + EPS)


def _rmsnorm_kernel(x_ref, g_ref, o_ref):
    xf = x_ref[...].astype(F32)
    o_ref[...] = (xf * _rms_scale(xf) * g_ref[...]).astype(o_ref.dtype)


def _rmsnorm(x, g, out_dtype, tm):
    m, d = x.shape
    return pl.pallas_call(
        _rmsnorm_kernel,
        grid=(m // tm,),
        in_specs=[pl.BlockSpec((tm, d), lambda i: (i, 0)),
                  pl.BlockSpec((1, d), lambda i: (0, 0))],
        out_specs=pl.BlockSpec((tm, d), lambda i: (i, 0)),
        out_shape=jax.ShapeDtypeStruct((m, d), out_dtype),
        compiler_params=_cparams("arbitrary"),
        name="rmsnorm",
    )(x, g.reshape(1, d).astype(F32))


def _mm_kernel(a_ref, b_ref, o_ref):
    o_ref[...] = jnp.dot(a_ref[...], b_ref[...],
                         preferred_element_type=F32).astype(o_ref.dtype)


def _mm_res_kernel(a_ref, b_ref, r_ref, o_ref):
    acc = jnp.dot(a_ref[...], b_ref[...], preferred_element_type=F32)
    o_ref[...] = (acc + r_ref[...]).astype(o_ref.dtype)


def _mm_res_kparts_kernel(a_ref, *rest, k_parts):
    b_refs, (r_ref, o_ref) = rest[:k_parts], rest[k_parts:]
    kc = a_ref.shape[1] // k_parts
    acc = None
    for p, b_ref in enumerate(b_refs):
        part = jnp.dot(a_ref[:, pl.ds(p * kc, kc)], b_ref[...], preferred_element_type=F32)
        acc = part if acc is None else acc + part
    o_ref[...] = (acc + r_ref[...]).astype(o_ref.dtype)


def _matmul(a, b, *, tm, tn, out_dtype, residual=None, name="matmul", k_parts=1):
    m, k = a.shape
    _, n = b.shape
    if k_parts > 1:
        kc = k // k_parts
        b_specs = [pl.BlockSpec((kc, tn), lambda i, j, p=p: (p, j)) for p in range(k_parts)]
        return pl.pallas_call(
            functools.partial(_mm_res_kparts_kernel, k_parts=k_parts),
            grid=(m // tm, n // tn),
            in_specs=[pl.BlockSpec((tm, k), lambda i, j: (i, 0))] + b_specs
            + [pl.BlockSpec((tm, tn), lambda i, j: (i, j))],
            out_specs=pl.BlockSpec((tm, tn), lambda i, j: (i, j)),
            out_shape=jax.ShapeDtypeStruct((m, n), out_dtype),
            compiler_params=_cparams("arbitrary", "arbitrary"),
            name=name,
        )(a, *([b] * k_parts), residual)
    in_specs = [pl.BlockSpec((tm, k), lambda i, j: (i, 0)),
                pl.BlockSpec((k, tn), lambda i, j: (0, j))]
    args = [a, b]
    kern = _mm_kernel
    if residual is not None:
        in_specs.append(pl.BlockSpec((tm, tn), lambda i, j: (i, j)))
        args.append(residual)
        kern = _mm_res_kernel
    return pl.pallas_call(
        kern,
        grid=(m // tm, n // tn),
        in_specs=in_specs,
        out_specs=pl.BlockSpec((tm, tn), lambda i, j: (i, j)),
        out_shape=jax.ShapeDtypeStruct((m, n), out_dtype),
        compiler_params=_cparams("arbitrary", "arbitrary"),
        name=name,
    )(*args)


def _mm_res_norm_kernel(a_ref, b_ref, r_ref, g_ref, o_ref, xg_ref, ssq_ref):
    x = jnp.dot(a_ref[...], b_ref[...], preferred_element_type=F32) + r_ref[...]
    o_ref[...] = x
    xg_ref[...] = (x * g_ref[...]).astype(xg_ref.dtype)
    ssq_ref[0] = jnp.sum(x * x, axis=-1, keepdims=True)


def _matmul_res_norm(a, b, residual, gain, *, tm, tn, name):
    m, k = a.shape
    _, n = b.shape
    nj = n // tn
    return pl.pallas_call(
        _mm_res_norm_kernel,
        grid=(m // tm, nj),
        in_specs=[pl.BlockSpec((tm, k), lambda i, j: (i, 0)),
                  pl.BlockSpec((k, tn), lambda i, j: (0, j)),
                  pl.BlockSpec((tm, tn), lambda i, j: (i, j)),
                  pl.BlockSpec((1, tn), lambda i, j: (0, j))],
        out_specs=[pl.BlockSpec((tm, tn), lambda i, j: (i, j)),
                   pl.BlockSpec((tm, tn), lambda i, j: (i, j)),
                   pl.BlockSpec((1, tm, 1), lambda i, j: (j, i, 0))],
        out_shape=[jax.ShapeDtypeStruct((m, n), F32),
                   jax.ShapeDtypeStruct((m, n), BF16),
                   jax.ShapeDtypeStruct((nj, m, 1), F32)],
        compiler_params=_cparams("arbitrary", "arbitrary"),
        name=name,
    )(a, b, residual, gain.reshape(1, n).astype(F32))


def _row_scale_kernel(ssq_ref, o_ref, *, width):
    ms = jnp.sum(ssq_ref[...], axis=0) * (1.0 / width)
    o_ref[...] = jnp.broadcast_to(lax.rsqrt(ms + EPS), o_ref.shape)


def _row_scale(ssq, width, *, tm):
    nj, m, _ = ssq.shape
    return pl.pallas_call(
        functools.partial(_row_scale_kernel, width=width),
        grid=(m // tm,),
        in_specs=[pl.BlockSpec((nj, tm, 1), lambda i: (0, i, 0))],
        out_specs=pl.BlockSpec((tm, LANES), lambda i: (i, 0)),
        out_shape=jax.ShapeDtypeStruct((m, LANES), F32),
        compiler_params=_cparams("arbitrary"),
        name="row_scale",
    )(ssq)


def _cmul(ar, ai, br, bi):
    return ar * br - ai * bi, ar * bi + ai * br


def _shift_rows_down(x, k):
    rows = lax.broadcasted_iota(jnp.int32, x.shape, 0)
    return jnp.where(rows >= k, pltpu.roll(x, k, 0), 0.0)


def _ssm_kernel(u_ref, perm_ref, unperm_ref, wb_ref, wc_ref, cp_ref, d_ref, z_ref,
                s_all_ref, carry_ref, *, units_per_step, steps_per_seq):
    sl = STATE_LANES
    re = pl.ds(0, sl)
    im = pl.ds(sl, sl)

    @pl.when(pl.program_id(1) % steps_per_seq == 0)
    def _():
        carry_ref[...] = jnp.zeros_like(carry_ref)

    def rows_of(i):
        return pl.ds(i * SUBLANES, SUBLANES)

    def const(k):
        rows = pl.ds(k * SUBLANES, SUBLANES)
        return cp_ref[0, rows, re], cp_ref[0, rows, im]

    base = SCAN_SUB

    a_re, a_im = const(0)
    c_re = carry_ref[:, re]
    c_im = carry_ref[:, im]
    def project_in(un):
        u = jnp.dot(perm_ref[...], u_ref[pl.ds(un * SCAN_UNIT, SCAN_UNIT), :],
                    preferred_element_type=F32).astype(BF16)
        s_all_ref[un] = jnp.dot(u, wb_ref[0], preferred_element_type=F32)
        return u

    u_next = project_in(0)
    for un in range(units_per_step):
        s_ref = s_all_ref.at[un]
        u = u_next
        if un + 1 < units_per_step:
            u_next = project_in(un + 1)

        h_re = jnp.zeros((SUBLANES, sl), F32)
        h_im = jnp.zeros((SUBLANES, sl), F32)
        for i in range(SCAN_SUB):
            p_re, p_im = _cmul(a_re, a_im, h_re, h_im)
            h_re = p_re + s_ref[rows_of(i), re]
            h_im = p_im + s_ref[rows_of(i), im]
            s_ref[rows_of(i), re] = h_re
            s_ref[rows_of(i), im] = h_im

        f_re, f_im = h_re, h_im
        for lvl, k in enumerate((1, 2, 4)):
            q_re, q_im = _cmul(*const(base + 1 + lvl),
                               _shift_rows_down(f_re, k), _shift_rows_down(f_im, k))
            f_re = f_re + q_re
            f_im = f_im + q_im
        in_re, in_im = _cmul(*const(base), c_re, c_im)
        in_re = in_re + _shift_rows_down(f_re, 1)
        in_im = in_im + _shift_rows_down(f_im, 1)
        n_re, n_im = _cmul(*const(base + 4), c_re, c_im)
        last = SUBLANES - 1
        c_re = n_re + jnp.broadcast_to(f_re[last:last + 1], (SUBLANES, sl))
        c_im = n_im + jnp.broadcast_to(f_im[last:last + 1], (SUBLANES, sl))

        parts = []
        for i in range(0, SCAN_SUB, 2):
            rows = []
            for ii in (i, i + 1):
                p_re, p_im = _cmul(*const(ii), in_re, in_im)
                rows.append(jnp.concatenate([s_ref[rows_of(ii), re] + p_re,
                                             s_ref[rows_of(ii), im] + p_im], axis=1))
            parts.append(jnp.concatenate(rows, axis=0).astype(BF16))
        y = jnp.dot(jnp.concatenate(parts, axis=0), wc_ref[0], preferred_element_type=F32)
        y = y + d_ref[0] * u.astype(F32)
        z = jax.nn.gelu(y).astype(BF16)
        z_ref[pl.ds(un * SCAN_UNIT, SCAN_UNIT), :] = jnp.dot(
            unperm_ref[...], z, preferred_element_type=F32).astype(z_ref.dtype)
    carry_ref[:, re] = c_re
    carry_ref[:, im] = c_im


def _ssm(proj, wb, wc, cp, d, *, seq_len, tt):
    m = proj.shape[0]
    nb = wb.shape[0]
    kern = functools.partial(_ssm_kernel, units_per_step=tt // SCAN_UNIT,
                             steps_per_seq=seq_len // tt)
    r = jnp.arange(SCAN_UNIT)
    src = (r % SUBLANES) * SCAN_SUB + r // SUBLANES
    perm = (src[:, None] == r[None, :]).astype(BF16)
    return pl.pallas_call(
        kern,
        grid=(nb, m // tt),
        in_specs=[pl.BlockSpec((tt, LANES), lambda b, t: (t, b)),
                  pl.BlockSpec((SCAN_UNIT, SCAN_UNIT), lambda b, t: (0, 0)),
                  pl.BlockSpec((SCAN_UNIT, SCAN_UNIT), lambda b, t: (0, 0)),
                  pl.BlockSpec((1, LANES, 2 * STATE_LANES), lambda b, t: (b, 0, 0)),
                  pl.BlockSpec((1, 2 * STATE_LANES, LANES), lambda b, t: (b, 0, 0)),
                  pl.BlockSpec((1,) + cp.shape[1:], lambda b, t: (b, 0, 0)),
                  pl.BlockSpec((1, 1, LANES), lambda b, t: (b, 0, 0))],
        out_specs=pl.BlockSpec((tt, LANES), lambda b, t: (t, b)),
        out_shape=jax.ShapeDtypeStruct((m, nb * LANES), BF16),
        scratch_shapes=[pltpu.VMEM((tt // SCAN_UNIT, SCAN_UNIT, 2 * STATE_LANES), F32),
                        pltpu.VMEM((SUBLANES, 2 * STATE_LANES), F32)],
        compiler_params=_cparams("arbitrary", "arbitrary"),
        name="s5_scan",
    )(proj, perm, perm.T, wb, wc, cp, d)


def _ssm_params(lam_re, lam_im, log_step, b_re, b_im, c_re, c_im, d):
    g, p = lam_re.shape
    h = b_re.shape[-1]
    nb = g // GROUPS_PER_BLOCK
    lr = lam_re.astype(F32)
    li = lam_im.astype(F32)
    step = jnp.exp(log_step.astype(F32))[:, None]
    mag = jnp.exp(lr * step)
    ang = li * step
    abar_re, abar_im = mag * jnp.cos(ang), mag * jnp.sin(ang)
    den = lr * lr + li * li
    nr, ni = abar_re - 1.0, abar_im
    coef_re = (nr * lr + ni * li) / den
    coef_im = (ni * lr - nr * li) / den
    br_f, bi_f = b_re.astype(F32), b_im.astype(F32)
    bbar_re = coef_re[..., None] * br_f - coef_im[..., None] * bi_f
    bbar_im = coef_re[..., None] * bi_f + coef_im[..., None] * br_f
    eye = jnp.eye(GROUPS_PER_BLOCK, dtype=F32)

    def in_block(x):
        x = x.reshape(nb, GROUPS_PER_BLOCK, p, h).transpose(0, 1, 3, 2)
        x = jnp.einsum('lghp,gk->lghkp', x, eye)
        return x.reshape(nb, LANES, STATE_LANES)

    def out_block(x):
        x = x.reshape(nb, GROUPS_PER_BLOCK, h, p).transpose(0, 1, 3, 2)
        x = jnp.einsum('lgph,gk->lgpkh', x, eye)
        return x.reshape(nb, STATE_LANES, LANES)

    wb = jnp.concatenate([in_block(bbar_re), in_block(bbar_im)], axis=-1).astype(BF16)
    wc = jnp.concatenate([out_block(c_re.astype(F32)), out_block(-c_im.astype(F32))],
                         axis=1).astype(BF16)

    a_re = abar_re.reshape(nb, STATE_LANES)
    a_im = abar_im.reshape(nb, STATE_LANES)
    pows = [(a_re, a_im)]
    for _ in range(SCAN_SUB - 1):
        pows.append(_cmul(a_re, a_im, *pows[-1]))
    sub = pows[-1]
    one = (jnp.ones_like(a_re), jnp.zeros_like(a_im))
    per_sublane = [one]
    for _ in range(SUBLANES - 1):
        per_sublane.append(_cmul(*sub, *per_sublane[-1]))
    sub2 = _cmul(*sub, *sub)
    sub4 = _cmul(*sub2, *sub2)
    sub8 = _cmul(*sub4, *sub4)

    def rep(c):
        row = jnp.concatenate(c, axis=-1)[:, None, :]
        return jnp.broadcast_to(row, (nb, SUBLANES, 2 * STATE_LANES))

    sub_rows = jnp.stack([jnp.concatenate(c, axis=-1) for c in per_sublane], axis=1)
    cp = jnp.concatenate([rep(c) for c in pows] + [sub_rows]
                         + [rep(c) for c in (sub, sub2, sub4, sub8)], axis=1)
    return wb, wc, cp, d.astype(F32).reshape(nb, 1, LANES)


def _mix_kernel(z_ref, gw_ref, gb_ref, ng_s_ref, pb_ref, pc_ref, pv_ref, hc_ref, hv_ref,
                cw_ref, ng_c_ref, o_ref, w_ref, *, tm, width, tiles_per_seq):
    z = z_ref[...]
    lin = jnp.dot(z, gw_ref[...], preferred_element_type=F32) + gb_ref[...]
    y = z.astype(F32) * jax.nn.sigmoid(lin)
    o_ref[:, pl.ds(0, width)] = (y * _rms_scale(y) * ng_s_ref[...]).astype(o_ref.dtype)

    cv = pc_ref[...].astype(F32) * pv_ref[...].astype(F32)
    halo = hc_ref[...].astype(F32) * hv_ref[...].astype(F32)
    first = pl.program_id(0) % tiles_per_seq == 0
    w_ref[pl.ds(0, SUBLANES), :] = jnp.where(first, 0.0, halo)
    w_ref[pl.ds(SUBLANES, tm), :] = cv
    conv = (cw_ref[0:1, :] * w_ref[pl.ds(SUBLANES - 2, tm), :]
            + cw_ref[1:2, :] * w_ref[pl.ds(SUBLANES - 1, tm), :]
            + cw_ref[2:3, :] * cv)
    yc = pb_ref[...].astype(F32) * conv
    o_ref[:, pl.ds(width, width)] = (yc * _rms_scale(yc) * ng_c_ref[...]).astype(o_ref.dtype)


def _mix(z, proj, glu_w, glu_b, ng_s, conv_w, ng_c, *, seq_len, tm):
    m, width = z.shape
    cb = width // width
    halo_blocks = tm // SUBLANES
    kern = functools.partial(_mix_kernel, tm=tm, width=width, tiles_per_seq=seq_len // tm)
    row = lambda a: a.reshape(1, width).astype(F32)

    def halo_map(col):
        return lambda i: (jnp.maximum(i * halo_blocks - 1, 0), col)

    return pl.pallas_call(
        kern,
        grid=(m // tm,),
        in_specs=[pl.BlockSpec((tm, width), lambda i: (i, 0)),
                  pl.BlockSpec((width, width), lambda i: (0, 0)),
                  pl.BlockSpec((1, width), lambda i: (0, 0)),
                  pl.BlockSpec((1, width), lambda i: (0, 0)),
                  pl.BlockSpec((tm, width), lambda i: (i, 1 * cb)),
                  pl.BlockSpec((tm, width), lambda i: (i, 2 * cb)),
                  pl.BlockSpec((tm, width), lambda i: (i, 3 * cb)),
                  pl.BlockSpec((SUBLANES, width), halo_map(2 * cb)),
                  pl.BlockSpec((SUBLANES, width), halo_map(3 * cb)),
                  pl.BlockSpec((3, width), lambda i: (0, 0)),
                  pl.BlockSpec((1, width), lambda i: (0, 0))],
        out_specs=pl.BlockSpec((tm, 2 * width), lambda i: (i, 0)),
        out_shape=jax.ShapeDtypeStruct((m, 2 * width), BF16),
        scratch_shapes=[pltpu.VMEM((tm + SUBLANES, width), F32)],
        compiler_params=_cparams("arbitrary"),
        name="mixer_tail",
    )(z, glu_w, row(glu_b), row(ng_s), proj, proj, proj, proj, proj,
      conv_w.astype(F32), row(ng_c))


def _xattn_kernel(hq_ref, rs_ref, wq_ref, k_ref, v_ref, o_ref, *, scale):
    q = jnp.dot(hq_ref[...], wq_ref[...], preferred_element_type=F32)
    s = lax.dot_general(q.astype(BF16), k_ref[...], (((1,), (1,)), ((), ())),
                        preferred_element_type=F32)
    rs = rs_ref[...] * scale
    s = s * jnp.concatenate([rs] * (s.shape[1] // LANES), axis=1)
    s = s - jnp.max(s, axis=-1, keepdims=True)
    e = jnp.exp(s)
    p = e / jnp.sum(e, axis=-1, keepdims=True)
    o_ref[...] = jnp.dot(p.astype(BF16), v_ref[...],
                         preferred_element_type=F32).astype(o_ref.dtype)


def _xattn(hq, row_scale, wq, k, v, *, seq_len, n_mem, tm):
    m, d = hq.shape
    hd = d // XATTN_HEADS
    tiles_per_seq = seq_len // tm
    kern = functools.partial(_xattn_kernel, scale=float(hd) ** -0.5)
    return pl.pallas_call(
        kern,
        grid=(m // tm, XATTN_HEADS),
        in_specs=[pl.BlockSpec((tm, d), lambda i, h: (i, 0)),
                  pl.BlockSpec((tm, LANES), lambda i, h: (i, 0)),
                  pl.BlockSpec((d, hd), lambda i, h: (0, h)),
                  pl.BlockSpec((n_mem, hd), lambda i, h: (i // tiles_per_seq, h)),
                  pl.BlockSpec((n_mem, hd), lambda i, h: (i // tiles_per_seq, h))],
        out_specs=pl.BlockSpec((tm, hd), lambda i, h: (i, h)),
        out_shape=jax.ShapeDtypeStruct((m, d), BF16),
        compiler_params=_cparams("arbitrary", "arbitrary"),
        name="xattn",
    )(hq, row_scale, wq, k, v)


def _ffn_up_kernel(h0_ref, h1_ref, rs_ref, wa_ref, wg_ref, cw_ref, cb_ref, o_ref,
                   wab_ref, wgb_ref, a0_ref, g0_ref, a1_ref, g1_ref, stage_ref, halo_ref,
                   *, tm, n_i, n_steps, tiles_per_seq, n_phase):
    t = pl.program_id(0)
    i = t % n_i

    @pl.when(jnp.logical_and(i == 0, t < n_steps - 1))
    def _():
        wab_ref[...] = wa_ref[...].astype(BF16)
        wgb_ref[...] = wg_ref[...].astype(BF16)

    mc = tm // n_phase
    rc = 16
    n_cols = o_ref.shape[1] // LANES
    tail_seq_start = (jnp.maximum(t - 1, 0) % n_i) % tiles_per_seq == 0

    def tail_piece(a_prev, g_prev, r0, col, above, zero):
        lanes = pl.ds(col * LANES, LANES)
        rows = pl.ds(r0, rc)
        a = a_prev[rows, lanes]
        if zero is not None:
            a = lax.bitcast_convert_type(lax.bitcast_convert_type(a, jnp.uint32) + zero, F32)
        rs = rs_ref[rows, :]
        a = a * rs
        ext = jnp.concatenate([above, a], axis=0)
        conv = (cw_ref[0:1, lanes] * ext[SUBLANES - 2:SUBLANES - 2 + rc]
                + cw_ref[1:2, lanes] * ext[SUBLANES - 1:SUBLANES - 1 + rc]
                + cw_ref[2:3, lanes] * a) + cb_ref[:, lanes]
        out = jax.nn.silu(conv) * (g_prev[rows, lanes] * rs)
        stage_ref[rows, lanes] = out.astype(stage_ref.dtype)
        bits = lax.bitcast_convert_type(out, jnp.uint32)
        return a[rc - SUBLANES:], (bits >> 16) >> 16

    def step(cur, prev, dots, tail):
        halves = (h0_ref, h1_ref)
        per_half = n_phase // len(halves)
        kc = h0_ref.shape[1] // per_half
        a_acc = g_acc = None
        zeros = [None] * n_cols
        if tail:
            above = [jnp.where(tail_seq_start, 0.0, halo_ref[col]) for col in range(n_cols)]
        for p in range(n_phase):
            if dots:
                hb = halves[p // per_half][:, pl.ds((p % per_half) * kc, kc)]
                pa = jnp.dot(hb, wab_ref[pl.ds(p * kc, kc), :], preferred_element_type=F32)
                pg = jnp.dot(hb, wgb_ref[pl.ds(p * kc, kc), :], preferred_element_type=F32)
                a_acc = pa if a_acc is None else a_acc + pa
                g_acc = pg if g_acc is None else g_acc + pg
            if tail:
                for r0 in range(p * mc, (p + 1) * mc, rc):
                    for col in range(n_cols):
                        above[col], zeros[col] = tail_piece(prev[0], prev[1], r0, col,
                                                            above[col], zeros[col])
        if tail:
            for col in range(n_cols):
                halo_ref[col] = above[col]
            o_ref[...] = stage_ref[...]
        if dots:
            cur[0][...] = a_acc
            cur[1][...] = g_acc

    slots = ((a0_ref, g0_ref), (a1_ref, g1_ref))
    middle = jnp.logical_and(t > 0, t < n_steps - 1)

    @pl.when(t == 0)
    def _():
        halo_ref[...] = jnp.zeros_like(halo_ref)
        step(slots[0], None, dots=True, tail=False)

    for parity in (0, 1):
        @pl.when(jnp.logical_and(middle, t % 2 == parity))
        def _():
            step(slots[parity], slots[1 - parity], dots=True, tail=True)

    @pl.when(t == n_steps - 1)
    def _():
        step(None, slots[(n_steps - 2) % 2], dots=False, tail=True)


def _ffn_up(h, row_scale, w_up, conv_w, conv_b, *, seq_len, tm, tf):
    m, d = h.shape
    d_ff = conv_w.shape[-1]
    nf = d_ff // tf
    n_i = m // tm
    n_steps = nf * n_i + 1
    kern = functools.partial(_ffn_up_kernel, tm=tm, n_i=n_i, n_steps=n_steps, n_phase=4,
                             tiles_per_seq=seq_len // tm)

    def tile_j(t):
        return jnp.minimum(t, n_steps - 2) // n_i

    def tail_i(t):
        return jnp.maximum(t - 1, 0) % n_i

    def tail_j(t):
        return jnp.maximum(t - 1, 0) // n_i

    return pl.pallas_call(
        kern,
        grid=(n_steps,),
        in_specs=[pl.BlockSpec((tm, d // 2), lambda t: (jnp.minimum(t, n_steps - 2) % n_i, 0)),
                  pl.BlockSpec((tm, d // 2), lambda t: (jnp.minimum(t, n_steps - 2) % n_i, 1)),
                  pl.BlockSpec((tm, LANES), lambda t: (tail_i(t), 0)),
                  pl.BlockSpec((d, tf), lambda t: (0, tile_j(t))),
                  pl.BlockSpec((d, tf), lambda t: (0, tile_j(t) + nf)),
                  pl.BlockSpec((3, tf), lambda t: (0, tail_j(t))),
                  pl.BlockSpec((1, tf), lambda t: (0, tail_j(t)))],
        out_specs=pl.BlockSpec((tm, tf), lambda t: (tail_i(t), tail_j(t))),
        out_shape=jax.ShapeDtypeStruct((m, d_ff), BF16),
        scratch_shapes=[pltpu.VMEM((d, tf), BF16), pltpu.VMEM((d, tf), BF16),
                        pltpu.VMEM((tm, tf), F32), pltpu.VMEM((tm, tf), F32),
                        pltpu.VMEM((tm, tf), F32), pltpu.VMEM((tm, tf), F32),
                        pltpu.VMEM((tm, tf), BF16),
                        pltpu.VMEM((tf // LANES, SUBLANES, LANES), F32)],
        compiler_params=_cparams("arbitrary"),
        name="ffn_up",
    )(h, h, row_scale, w_up, w_up, conv_w.astype(F32), conv_b.reshape(1, d_ff).astype(F32))


def kernel(x, mem, norm_mix_g, w_in, ssm_lambda_re, ssm_lambda_im, ssm_log_step, ssm_b_re, ssm_b_im, ssm_c_re, ssm_c_im, ssm_d, ssm_glu_w, ssm_glu_b, conv_w, out_norm_ssm_g, out_norm_conv_g, w_out, norm_xattn_g, norm_mem_g, xattn_wq, xattn_wk, xattn_wv, xattn_wo, norm_ffn_g, ffn_w_up, ffn_conv_w, ffn_conv_b, ffn_w_down, norm_final_g):
    bsz, seq_len, d_model = x.shape
    n_mem = mem.shape[1]
    depth = w_in.shape[0]
    m = bsz * seq_len
    xs = x.reshape(m, d_model)
    mems = mem.reshape(bsz * n_mem, d_model)
    for l in range(depth):
        h = _rmsnorm(xs, norm_mix_g[l], BF16, tm=512)
        proj = _matmul(h, w_in[l].astype(BF16), tm=1024, tn=1024, out_dtype=BF16, name="proj_in")
        wb, wc, cp, d = _ssm_params(ssm_lambda_re[l], ssm_lambda_im[l], ssm_log_step[l],
                                    ssm_b_re[l], ssm_b_im[l], ssm_c_re[l], ssm_c_im[l], ssm_d[l])
        z = _ssm(proj, wb, wc, cp, d, seq_len=seq_len, tt=4096)
        mixed = _mix(z, proj, ssm_glu_w[l].astype(BF16), ssm_glu_b[l], out_norm_ssm_g[l],
                     conv_w[l], out_norm_conv_g[l], seq_len=seq_len, tm=256)
        xs, hq, ssq = _matmul_res_norm(mixed, w_out[l].astype(BF16), xs, norm_xattn_g[l],
                                       tm=1024, tn=512, name="proj_out")
        hm = _rmsnorm(mems, norm_mem_g[l], BF16, tm=512)
        k = _matmul(hm, xattn_wk[l].astype(BF16), tm=1024, tn=1024, out_dtype=BF16, name="proj_k")
        v = _matmul(hm, xattn_wv[l].astype(BF16), tm=1024, tn=1024, out_dtype=BF16, name="proj_v")
        o = _xattn(hq, _row_scale(ssq, d_model, tm=1024), xattn_wq[l].astype(BF16), k, v,
                   seq_len=seq_len, n_mem=n_mem, tm=1024)
        xs, h, ssq = _matmul_res_norm(o, xattn_wo[l].astype(BF16), xs, norm_ffn_g[l],
                                      tm=1024, tn=512, name="proj_o")
        act = _ffn_up(h, _row_scale(ssq, d_model, tm=1024), ffn_w_up[l], ffn_conv_w[l],
                      ffn_conv_b[l], seq_len=seq_len, tm=1024, tf=256)
        xs = _matmul(act, ffn_w_down[l].astype(BF16), tm=512, tn=512, out_dtype=F32,
                     residual=xs, name="ffn_down", k_parts=2)
    out = _rmsnorm(xs, norm_final_g, x.dtype, tm=512)
    return out.reshape(bsz, seq_len, d_model)
```

```python
import functools

import jax
import jax.numpy as jnp
from jax import lax
from jax.experimental import pallas as pl
from jax.experimental.pallas import tpu as pltpu

F32 = jnp.float32
BF16 = jnp.bfloat16

EPS = 1e-6
SSM_GROUP = 16
SSM_STATE = 64
XATTN_HEADS = 4
V7X_VMEM_LIMIT_BYTES = 56 * 1024 * 1024

LANES = 128
SUBLANES = 8
GROUPS_PER_BLOCK = LANES // SSM_GROUP
STATE_LANES = GROUPS_PER_BLOCK * SSM_STATE
SCAN_SUB = 32
SCAN_UNIT = SUBLANES * SCAN_SUB


def _cparams(*sem, flags=None):
    return pltpu.CompilerParams(dimension_semantics=sem, flags=flags,
                                vmem_limit_bytes=V7X_VMEM_LIMIT_BYTES)


def _rms_scale(xf):
    return lax.rsqrt(jnp.mean(xf * xf, axis=-1, keepdims=True) + EPS)


def _rmsnorm_kernel(x_ref, g_ref, o_ref):
    xf = x_ref[...].astype(F32)
    o_ref[...] = (xf * _rms_scale(xf) * g_ref[...]).astype(o_ref.dtype)


def _rmsnorm(x, g, out_dtype, tm):
    m, d = x.shape
    return pl.pallas_call(
        _rmsnorm_kernel,
        grid=(m // tm,),
        in_specs=[pl.BlockSpec((tm, d), lambda i: (i, 0)),
                  pl.BlockSpec((1, d), lambda i: (0, 0))],
        out_specs=pl.BlockSpec((tm, d), lambda i: (i, 0)),
        out_shape=jax.ShapeDtypeStruct((m, d), out_dtype),
        compiler_params=_cparams("arbitrary"),
        name="rmsnorm",
    )(x, g.reshape(1, d).astype(F32))


def _mm_kernel(a_ref, b_ref, o_ref):
    o_ref[...] = jnp.dot(a_ref[...], b_ref[...],
                         preferred_element_type=F32).astype(o_ref.dtype)


def _mm_res_kernel(a_ref, b_ref, r_ref, o_ref):
    acc = jnp.dot(a_ref[...], b_ref[...], preferred_element_type=F32)
    o_ref[...] = (acc + r_ref[...]).astype(o_ref.dtype)


def _mm_castb_kernel(a_ref, b_ref, o_ref):
    o_ref[...] = jnp.dot(a_ref[...], b_ref[...].astype(BF16),
                         preferred_element_type=F32).astype(o_ref.dtype)


def _matmul(a, b, *, tm, tn, out_dtype, residual=None, name="matmul"):
    m, k = a.shape
    _, n = b.shape
    in_specs = [pl.BlockSpec((tm, k), lambda i, j: (i, 0)),
                pl.BlockSpec((k, tn), lambda i, j: (0, j))]
    args = [a, b]
    kern = _mm_kernel if b.dtype == BF16 else _mm_castb_kernel
    if residual is not None:
        assert b.dtype == BF16
        in_specs.append(pl.BlockSpec((tm, tn), lambda i, j: (i, j)))
        args.append(residual)
        kern = _mm_res_kernel
    return pl.pallas_call(
        kern,
        grid=(m // tm, n // tn),
        in_specs=in_specs,
        out_specs=pl.BlockSpec((tm, tn), lambda i, j: (i, j)),
        out_shape=jax.ShapeDtypeStruct((m, n), out_dtype),
        compiler_params=_cparams("arbitrary", "arbitrary"),
        name=name,
    )(*args)


def _mm_res_norm_kernel(a_ref, b_ref, r_ref, g_ref, o_ref, xg_ref, ssq_ref):
    x = jnp.dot(a_ref[...], b_ref[...], preferred_element_type=F32) + r_ref[...]
    o_ref[...] = x
    xg_ref[...] = (x * g_ref[...]).astype(xg_ref.dtype)
    part = jnp.broadcast_to(jnp.sum(x * x, axis=-1, keepdims=True), ssq_ref.shape)

    @pl.when(pl.program_id(1) == 0)
    def _():
        ssq_ref[...] = part

    @pl.when(pl.program_id(1) > 0)
    def _():
        ssq_ref[...] += part


def _matmul_res_norm(a, b, residual, gain, *, tm, tn, name):
    m, k = a.shape
    _, n = b.shape
    return pl.pallas_call(
        _mm_res_norm_kernel,
        grid=(m // tm, n // tn),
        in_specs=[pl.BlockSpec((tm, k), lambda i, j: (i, 0)),
                  pl.BlockSpec((k, tn), lambda i, j: (0, j)),
                  pl.BlockSpec((tm, tn), lambda i, j: (i, j)),
                  pl.BlockSpec((1, tn), lambda i, j: (0, j))],
        out_specs=[pl.BlockSpec((tm, tn), lambda i, j: (i, j)),
                   pl.BlockSpec((tm, tn), lambda i, j: (i, j)),
                   pl.BlockSpec((tm, LANES), lambda i, j: (i, 0))],
        out_shape=[jax.ShapeDtypeStruct((m, n), F32),
                   jax.ShapeDtypeStruct((m, n), BF16),
                   jax.ShapeDtypeStruct((m, LANES), F32)],
        compiler_params=_cparams("arbitrary", "arbitrary"),
        name=name,
    )(a, b, residual, gain.reshape(1, n).astype(F32))


def _row_scale(ssq, width):
    return lax.rsqrt(ssq * (1.0 / width) + EPS)


def _cmul(ar, ai, br, bi):
    return ar * br - ai * bi, ar * bi + ai * br


def _shift_rows_down(x, k):
    rows = lax.broadcasted_iota(jnp.int32, x.shape, 0)
    return jnp.where(rows >= k, pltpu.roll(x, k, 0), 0.0)


def _ssm_kernel(u_ref, perm_ref, unperm_ref, wb_ref, wc_ref, cp_ref, d_ref, z_ref,
                s_all_ref, carry_ref, *, units_per_step, steps_per_seq):
    sl = STATE_LANES
    re = pl.ds(0, sl)
    im = pl.ds(sl, sl)

    @pl.when(pl.program_id(1) % steps_per_seq == 0)
    def _():
        carry_ref[...] = jnp.zeros_like(carry_ref)

    def rows_of(i):
        return pl.ds(i * SUBLANES, SUBLANES)

    def const(k):
        rows = pl.ds(k * SUBLANES, SUBLANES)
        return cp_ref[0, rows, re], cp_ref[0, rows, im]

    base = SCAN_SUB

    a_re, a_im = const(0)
    c_re = carry_ref[:, re]
    c_im = carry_ref[:, im]
    def project_in(un):
        u = jnp.dot(perm_ref[...], u_ref[pl.ds(un * SCAN_UNIT, SCAN_UNIT), :],
                    preferred_element_type=F32).astype(BF16)
        s_all_ref[un] = jnp.dot(u, wb_ref[0], preferred_element_type=F32)
        return u

    u_next = project_in(0)
    for un in range(units_per_step):
        s_ref = s_all_ref.at[un]
        u = u_next
        if un + 1 < units_per_step:
            u_next = project_in(un + 1)

        h_re = jnp.zeros((SUBLANES, sl), F32)
        h_im = jnp.zeros((SUBLANES, sl), F32)
        for i in range(SCAN_SUB):
            p_re, p_im = _cmul(a_re, a_im, h_re, h_im)
            h_re = p_re + s_ref[rows_of(i), re]
            h_im = p_im + s_ref[rows_of(i), im]
            s_ref[rows_of(i), re] = h_re
            s_ref[rows_of(i), im] = h_im

        f_re, f_im = h_re, h_im
        for lvl, k in enumerate((1, 2, 4)):
            q_re, q_im = _cmul(*const(base + lvl),
                               _shift_rows_down(f_re, k), _shift_rows_down(f_im, k))
            f_re = f_re + q_re
            f_im = f_im + q_im
        in_re, in_im = _cmul(*const(base + 4), c_re, c_im)
        in_re = in_re + _shift_rows_down(f_re, 1)
        in_im = in_im + _shift_rows_down(f_im, 1)
        n_re, n_im = _cmul(*const(base + 3), c_re, c_im)
        last = SUBLANES - 1
        c_re = n_re + jnp.broadcast_to(f_re[last:last + 1], (SUBLANES, sl))
        c_im = n_im + jnp.broadcast_to(f_im[last:last + 1], (SUBLANES, sl))

        parts = []
        for i in range(0, SCAN_SUB, 2):
            rows = []
            for ii in (i, i + 1):
                p_re, p_im = _cmul(*const(ii), in_re, in_im)
                rows.append(jnp.concatenate([s_ref[rows_of(ii), re] + p_re,
                                             s_ref[rows_of(ii), im] + p_im], axis=1))
            parts.append(jnp.concatenate(rows, axis=0).astype(BF16))
        y = jnp.dot(jnp.concatenate(parts, axis=0), wc_ref[0], preferred_element_type=F32)
        y = y + d_ref[0] * u.astype(F32)
        z = jax.nn.gelu(y).astype(BF16)
        z_ref[pl.ds(un * SCAN_UNIT, SCAN_UNIT), :] = jnp.dot(
            unperm_ref[...], z, preferred_element_type=F32).astype(z_ref.dtype)
    carry_ref[:, re] = c_re
    carry_ref[:, im] = c_im


def _ssm(proj, wb, wc, cp, d, *, seq_len, tt):
    m = proj.shape[0]
    nb = wb.shape[0]
    kern = functools.partial(_ssm_kernel, units_per_step=tt // SCAN_UNIT,
                             steps_per_seq=seq_len // tt)
    r = jnp.arange(SCAN_UNIT)
    src = (r % SUBLANES) * SCAN_SUB + r // SUBLANES
    perm = (src[:, None] == r[None, :]).astype(BF16)
    return pl.pallas_call(
        kern,
        grid=(nb, m // tt),
        in_specs=[pl.BlockSpec((tt, LANES), lambda b, t: (t, b)),
                  pl.BlockSpec((SCAN_UNIT, SCAN_UNIT), lambda b, t: (0, 0)),
                  pl.BlockSpec((SCAN_UNIT, SCAN_UNIT), lambda b, t: (0, 0)),
                  pl.BlockSpec((1, LANES, 2 * STATE_LANES), lambda b, t: (b, 0, 0)),
                  pl.BlockSpec((1, 2 * STATE_LANES, LANES), lambda b, t: (b, 0, 0)),
                  pl.BlockSpec((1,) + cp.shape[1:], lambda b, t: (b, 0, 0)),
                  pl.BlockSpec((1, 1, LANES), lambda b, t: (b, 0, 0))],
        out_specs=pl.BlockSpec((tt, LANES), lambda b, t: (t, b)),
        out_shape=jax.ShapeDtypeStruct((m, nb * LANES), BF16),
        scratch_shapes=[pltpu.VMEM((tt // SCAN_UNIT, SCAN_UNIT, 2 * STATE_LANES), F32),
                        pltpu.VMEM((SUBLANES, 2 * STATE_LANES), F32)],
        compiler_params=_cparams("arbitrary", "arbitrary"),
        name="s5_scan",
    )(proj, perm, perm.T, wb, wc, cp, d)


def _ssm_params(lam_re, lam_im, log_step, b_re, b_im, c_re, c_im, d):
    g, p = lam_re.shape
    h = b_re.shape[-1]
    nb = g // GROUPS_PER_BLOCK
    lr = lam_re.astype(F32)
    li = lam_im.astype(F32)
    step = jnp.exp(log_step.astype(F32))[:, None]
    mag = jnp.exp(lr * step)
    ang = li * step
    abar_re, abar_im = mag * jnp.cos(ang), mag * jnp.sin(ang)
    den = lr * lr + li * li
    nr, ni = abar_re - 1.0, abar_im
    coef_re = (nr * lr + ni * li) / den
    coef_im = (ni * lr - nr * li) / den
    br_f, bi_f = b_re.astype(F32), b_im.astype(F32)
    bbar_re = coef_re[..., None] * br_f - coef_im[..., None] * bi_f
    bbar_im = coef_re[..., None] * bi_f + coef_im[..., None] * br_f
    eye = jnp.eye(GROUPS_PER_BLOCK, dtype=F32)

    def in_block(x):
        x = x.reshape(nb, GROUPS_PER_BLOCK, p, h).transpose(0, 1, 3, 2)
        x = jnp.einsum('lghp,gk->lghkp', x, eye)
        return x.reshape(nb, LANES, STATE_LANES)

    def out_block(x):
        x = x.reshape(nb, GROUPS_PER_BLOCK, h, p).transpose(0, 1, 3, 2)
        x = jnp.einsum('lgph,gk->lgpkh', x, eye)
        return x.reshape(nb, STATE_LANES, LANES)

    wb = jnp.concatenate([in_block(bbar_re), in_block(bbar_im)], axis=-1).astype(BF16)
    wc = jnp.concatenate([out_block(c_re.astype(F32)), out_block(-c_im.astype(F32))],
                         axis=1).astype(BF16)

    a_re = abar_re.reshape(nb, STATE_LANES)
    a_im = abar_im.reshape(nb, STATE_LANES)
    pows = [(a_re, a_im)]
    for _ in range(SCAN_SUB - 1):
        pows.append(_cmul(a_re, a_im, *pows[-1]))
    sub = pows[-1]
    one = (jnp.ones_like(a_re), jnp.zeros_like(a_im))
    per_sublane = [one]
    for _ in range(SUBLANES - 1):
        per_sublane.append(_cmul(*sub, *per_sublane[-1]))
    sub2 = _cmul(*sub, *sub)
    sub4 = _cmul(*sub2, *sub2)
    sub8 = _cmul(*sub4, *sub4)

    def table(pairs):
        return jnp.stack([jnp.concatenate(c, axis=-1) for c in pairs], axis=1)

    repeated = jnp.repeat(table(pows + [sub, sub2, sub4, sub8]), SUBLANES, axis=1)
    cp = jnp.concatenate([repeated, table(per_sublane)], axis=1)
    return wb, wc, cp, d.astype(F32).reshape(nb, 1, LANES)


def _mix_kernel(z_ref, gw_ref, gb_ref, ng_s_ref, pb_ref, pc_ref, pv_ref, hc_ref, hv_ref,
                cw_ref, ng_c_ref, o_ref, w_ref, *, tm, width, tiles_per_seq):
    z = z_ref[...]
    lin = jnp.dot(z, gw_ref[...], preferred_element_type=F32) + gb_ref[...]
    y = z.astype(F32) * jax.nn.sigmoid(lin)
    o_ref[:, pl.ds(0, width)] = (y * _rms_scale(y) * ng_s_ref[...]).astype(o_ref.dtype)

    cv = pc_ref[...].astype(F32) * pv_ref[...].astype(F32)
    halo = hc_ref[...].astype(F32) * hv_ref[...].astype(F32)
    first = pl.program_id(0) % tiles_per_seq == 0
    w_ref[pl.ds(0, SUBLANES), :] = jnp.where(first, 0.0, halo)
    w_ref[pl.ds(SUBLANES, tm), :] = cv
    conv = (cw_ref[0:1, :] * w_ref[pl.ds(SUBLANES - 2, tm), :]
            + cw_ref[1:2, :] * w_ref[pl.ds(SUBLANES - 1, tm), :]
            + cw_ref[2:3, :] * cv)
    yc = pb_ref[...].astype(F32) * conv
    o_ref[:, pl.ds(width, width)] = (yc * _rms_scale(yc) * ng_c_ref[...]).astype(o_ref.dtype)


def _mix(z, proj, glu_w, glu_b, ng_s, conv_w, ng_c, *, seq_len, tm):
    m, width = z.shape
    cb = width // width
    halo_blocks = tm // SUBLANES
    kern = functools.partial(_mix_kernel, tm=tm, width=width, tiles_per_seq=seq_len // tm)
    row = lambda a: a.reshape(1, width).astype(F32)

    def halo_map(col):
        return lambda i: (jnp.maximum(i * halo_blocks - 1, 0), col)

    return pl.pallas_call(
        kern,
        grid=(m // tm,),
        in_specs=[pl.BlockSpec((tm, width), lambda i: (i, 0)),
                  pl.BlockSpec((width, width), lambda i: (0, 0)),
                  pl.BlockSpec((1, width), lambda i: (0, 0)),
                  pl.BlockSpec((1, width), lambda i: (0, 0)),
                  pl.BlockSpec((tm, width), lambda i: (i, 1 * cb)),
                  pl.BlockSpec((tm, width), lambda i: (i, 2 * cb)),
                  pl.BlockSpec((tm, width), lambda i: (i, 3 * cb)),
                  pl.BlockSpec((SUBLANES, width), halo_map(2 * cb)),
                  pl.BlockSpec((SUBLANES, width), halo_map(3 * cb)),
                  pl.BlockSpec((3, width), lambda i: (0, 0)),
                  pl.BlockSpec((1, width), lambda i: (0, 0))],
        out_specs=pl.BlockSpec((tm, 2 * width), lambda i: (i, 0)),
        out_shape=jax.ShapeDtypeStruct((m, 2 * width), BF16),
        scratch_shapes=[pltpu.VMEM((tm + SUBLANES, width), F32)],
        compiler_params=_cparams("arbitrary"),
        name="mixer_tail",
    )(z, glu_w, row(glu_b), row(ng_s), proj, proj, proj, proj, proj,
      conv_w.astype(F32), row(ng_c))


def _xattn_kernel(hq_ref, ssq_ref, wq_ref, k_ref, v_ref, o_ref, *, scale):
    q = jnp.dot(hq_ref[...], wq_ref[...], preferred_element_type=F32)
    s = lax.dot_general(q.astype(BF16), k_ref[...], (((1,), (1,)), ((), ())),
                        preferred_element_type=F32)
    rs = _row_scale(ssq_ref[...], hq_ref.shape[1]) * scale
    s = s * jnp.concatenate([rs] * (s.shape[1] // LANES), axis=1)
    s = s - jnp.max(s, axis=-1, keepdims=True)
    e = jnp.exp(s)
    p = e / jnp.sum(e, axis=-1, keepdims=True)
    o_ref[...] = jnp.dot(p.astype(BF16), v_ref[...],
                         preferred_element_type=F32).astype(o_ref.dtype)


def _xattn(hq, ssq, wq, k, v, *, seq_len, n_mem, tm):
    m, d = hq.shape
    hd = d // XATTN_HEADS
    tiles_per_seq = seq_len // tm
    kern = functools.partial(_xattn_kernel, scale=float(hd) ** -0.5)
    return pl.pallas_call(
        kern,
        grid=(m // tm, XATTN_HEADS),
        in_specs=[pl.BlockSpec((tm, d), lambda i, h: (i, 0)),
                  pl.BlockSpec((tm, LANES), lambda i, h: (i, 0)),
                  pl.BlockSpec((d, hd), lambda i, h: (0, h)),
                  pl.BlockSpec((n_mem, hd), lambda i, h: (i // tiles_per_seq, h)),
                  pl.BlockSpec((n_mem, hd), lambda i, h: (i // tiles_per_seq, h))],
        out_specs=pl.BlockSpec((tm, hd), lambda i, h: (i, h)),
        out_shape=jax.ShapeDtypeStruct((m, d), BF16),
        compiler_params=_cparams("arbitrary", "arbitrary"),
        name="xattn",
    )(hq, ssq, wq, k, v)


def _ffn_up_kernel(h_ref, ssq_ref, wa_ref, wg_ref, cw_ref, cb_ref, o_ref,
                   wab_ref, wgb_ref, a0_ref, g0_ref, a1_ref, g1_ref, stage_ref, halo_ref,
                   *, tm, n_i, n_steps, tiles_per_seq, n_phase):
    t = pl.program_id(0)
    i = t % n_i

    @pl.when(jnp.logical_and(i == 0, t < n_steps - 1))
    def _():
        wab_ref[...] = wa_ref[...].astype(BF16)
        wgb_ref[...] = wg_ref[...].astype(BF16)

    mc = tm // n_phase
    rc = 16
    n_cols = o_ref.shape[1] // LANES
    tail_seq_start = (jnp.maximum(t - 1, 0) % n_i) % tiles_per_seq == 0

    def tail_piece(a_prev, g_prev, r0, col, above, zero):
        lanes = pl.ds(col * LANES, LANES)
        rows = pl.ds(r0, rc)
        a = a_prev[rows, lanes]
        if zero is not None:
            a = lax.bitcast_convert_type(lax.bitcast_convert_type(a, jnp.uint32) + zero, F32)
        rs = _row_scale(ssq_ref[rows, :], h_ref.shape[1])
        a = a * rs
        ext = jnp.concatenate([above, a], axis=0)
        conv = (cw_ref[0:1, lanes] * ext[SUBLANES - 2:SUBLANES - 2 + rc]
                + cw_ref[1:2, lanes] * ext[SUBLANES - 1:SUBLANES - 1 + rc]
                + cw_ref[2:3, lanes] * a) + cb_ref[:, lanes]
        out = jax.nn.silu(conv) * (g_prev[rows, lanes] * rs)
        stage_ref[rows, lanes] = out.astype(stage_ref.dtype)
        bits = lax.bitcast_convert_type(out, jnp.uint32)
        return a[rc - SUBLANES:], (bits >> 16) >> 16

    def step(cur, prev, dots, tail):
        kc = h_ref.shape[1] // n_phase
        a_acc = g_acc = None
        zeros = [None] * n_cols
        if tail:
            above = [jnp.where(tail_seq_start, 0.0, halo_ref[col]) for col in range(n_cols)]
        for p in range(n_phase):
            if dots:
                hb = h_ref[:, pl.ds(p * kc, kc)]
                pa = jnp.dot(hb, wab_ref[pl.ds(p * kc, kc), :], preferred_element_type=F32)
                pg = jnp.dot(hb, wgb_ref[pl.ds(p * kc, kc), :], preferred_element_type=F32)
                a_acc = pa if a_acc is None else a_acc + pa
                g_acc = pg if g_acc is None else g_acc + pg
            if tail:
                for r0 in range(p * mc, (p + 1) * mc, rc):
                    for col in range(n_cols):
                        above[col], zeros[col] = tail_piece(prev[0], prev[1], r0, col,
                                                            above[col], zeros[col])
        if tail:
            for col in range(n_cols):
                halo_ref[col] = above[col]
            o_ref[...] = stage_ref[...]
        if dots:
            cur[0][...] = a_acc
            cur[1][...] = g_acc

    slots = ((a0_ref, g0_ref), (a1_ref, g1_ref))
    middle = jnp.logical_and(t > 0, t < n_steps - 1)

    @pl.when(t == 0)
    def _():
        halo_ref[...] = jnp.zeros_like(halo_ref)
        step(slots[0], None, dots=True, tail=False)

    for parity in (0, 1):
        @pl.when(jnp.logical_and(middle, t % 2 == parity))
        def _():
            step(slots[parity], slots[1 - parity], dots=True, tail=True)

    @pl.when(t == n_steps - 1)
    def _():
        step(None, slots[(n_steps - 2) % 2], dots=False, tail=True)


def _ffn_up(h, ssq, w_up, conv_w, conv_b, *, seq_len, tm, tf):
    m, d = h.shape
    d_ff = conv_w.shape[-1]
    nf = d_ff // tf
    n_i = m // tm
    n_steps = nf * n_i + 1
    kern = functools.partial(_ffn_up_kernel, tm=tm, n_i=n_i, n_steps=n_steps, n_phase=4,
                             tiles_per_seq=seq_len // tm)

    def tile_j(t):
        return jnp.minimum(t, n_steps - 2) // n_i

    def tail_i(t):
        return jnp.maximum(t - 1, 0) % n_i

    def tail_j(t):
        return jnp.maximum(t - 1, 0) // n_i

    return pl.pallas_call(
        kern,
        grid=(n_steps,),
        in_specs=[pl.BlockSpec((tm, d), lambda t: (jnp.minimum(t, n_steps - 2) % n_i, 0)),
                  pl.BlockSpec((tm, LANES), lambda t: (tail_i(t), 0)),
                  pl.BlockSpec((d, tf), lambda t: (0, tile_j(t))),
                  pl.BlockSpec((d, tf), lambda t: (0, tile_j(t) + nf)),
                  pl.BlockSpec((3, tf), lambda t: (0, tail_j(t))),
                  pl.BlockSpec((1, tf), lambda t: (0, tail_j(t)))],
        out_specs=pl.BlockSpec((tm, tf), lambda t: (tail_i(t), tail_j(t))),
        out_shape=jax.ShapeDtypeStruct((m, d_ff), BF16),
        scratch_shapes=[pltpu.VMEM((d, tf), BF16), pltpu.VMEM((d, tf), BF16),
                        pltpu.VMEM((tm, tf), F32), pltpu.VMEM((tm, tf), F32),
                        pltpu.VMEM((tm, tf), F32), pltpu.VMEM((tm, tf), F32),
                        pltpu.VMEM((tm, tf), BF16),
                        pltpu.VMEM((tf // LANES, SUBLANES, LANES), F32)],
        compiler_params=_cparams("arbitrary"),
        name="ffn_up",
    )(h, ssq, w_up, w_up, conv_w.astype(F32), conv_b.reshape(1, d_ff).astype(F32))


def kernel(x, mem, norm_mix_g, w_in, ssm_lambda_re, ssm_lambda_im, ssm_log_step, ssm_b_re, ssm_b_im, ssm_c_re, ssm_c_im, ssm_d, ssm_glu_w, ssm_glu_b, conv_w, out_norm_ssm_g, out_norm_conv_g, w_out, norm_xattn_g, norm_mem_g, xattn_wq, xattn_wk, xattn_wv, xattn_wo, norm_ffn_g, ffn_w_up, ffn_conv_w, ffn_conv_b, ffn_w_down, norm_final_g):
    bsz, seq_len, d_model = x.shape
    n_mem = mem.shape[1]
    depth = w_in.shape[0]
    m = bsz * seq_len
    xs = x.reshape(m, d_model)
    mems = mem.reshape(bsz * n_mem, d_model)
    for l in range(depth):
        h = _rmsnorm(xs, norm_mix_g[l], BF16, tm=512)
        proj = _matmul(h, w_in[l].astype(BF16), tm=1024, tn=1024, out_dtype=BF16, name="proj_in")
        wb, wc, cp, d = _ssm_params(ssm_lambda_re[l], ssm_lambda_im[l], ssm_log_step[l],
                                    ssm_b_re[l], ssm_b_im[l], ssm_c_re[l], ssm_c_im[l], ssm_d[l])
        z = _ssm(proj, wb, wc, cp, d, seq_len=seq_len, tt=4096)
        mixed = _mix(z, proj, ssm_glu_w[l].astype(BF16), ssm_glu_b[l], out_norm_ssm_g[l],
                     conv_w[l], out_norm_conv_g[l], seq_len=seq_len, tm=256)
        xs, hq, ssq = _matmul_res_norm(mixed, w_out[l].astype(BF16), xs, norm_xattn_g[l],
                                       tm=1024, tn=512, name="proj_out")
        hm = _rmsnorm(mems, norm_mem_g[l], BF16, tm=512)
        k = _matmul(hm, xattn_wk[l], tm=1024, tn=512, out_dtype=BF16, name="proj_k")
        v = _matmul(hm, xattn_wv[l], tm=1024, tn=512, out_dtype=BF16, name="proj_v")
        o = _xattn(hq, ssq, xattn_wq[l].astype(BF16), k, v,
                   seq_len=seq_len, n_mem=n_mem, tm=1024)
        xs, h, ssq = _matmul_res_norm(o, xattn_wo[l].astype(BF16), xs, norm_ffn_g[l],
                                      tm=1024, tn=512, name="proj_o")
        act = _ffn_up(h, ssq, ffn_w_up[l], ffn_conv_w[l], ffn_conv_b[l],
                      seq_len=seq_len, tm=1024, tf=256)
        xs = _matmul(act, ffn_w_down[l].astype(BF16), tm=512, tn=512, out_dtype=F32,
                     residual=xs, name="ffn_down")
    out = _rmsnorm(xs, norm_final_g, x.dtype, tm=512)
    return out.reshape(bsz, seq_len, d_model)
```

```python
import functools

import jax
import jax.numpy as jnp
from jax import lax
from jax.experimental import pallas as pl
from jax.experimental.pallas import tpu as pltpu

F32 = jnp.float32
BF16 = jnp.bfloat16

EPS = 1e-6
SSM_GROUP = 16
SSM_STATE = 64
XATTN_HEADS = 4
V7X_VMEM_LIMIT_BYTES = 56 * 1024 * 1024

LANES = 128
SUBLANES = 8
GROUPS_PER_BLOCK = LANES // SSM_GROUP
STATE_LANES = GROUPS_PER_BLOCK * SSM_STATE
SCAN_SUB = 32
SCAN_UNIT = SUBLANES * SCAN_SUB


def _cparams(*sem, flags=None):
    return pltpu.CompilerParams(dimension_semantics=sem, flags=flags,
                                vmem_limit_bytes=V7X_VMEM_LIMIT_BYTES)


def _rms_scale(xf):
    return lax.rsqrt(jnp.mean(xf * xf, axis=-1, keepdims=True) + EPS)


def _rmsnorm_kernel(x_ref, g_ref, o_ref):
    xf = x_ref[...].astype(F32)
    o_ref[...] = (xf * _rms_scale(xf) * g_ref[...]).astype(o_ref.dtype)


def _rmsnorm(x, g, out_dtype, tm):
    m, d = x.shape
    return pl.pallas_call(
        _rmsnorm_kernel,
        grid=(m // tm,),
        in_specs=[pl.BlockSpec((tm, d), lambda i: (i, 0)),
                  pl.BlockSpec((1, d), lambda i: (0, 0))],
        out_specs=pl.BlockSpec((tm, d), lambda i: (i, 0)),
        out_shape=jax.ShapeDtypeStruct((m, d), out_dtype),
        compiler_params=_cparams("arbitrary"),
        name="rmsnorm",
    )(x, g.reshape(1, d).astype(F32))


def _mm_kernel(a_ref, b_ref, o_ref):
    o_ref[...] = jnp.dot(a_ref[...], b_ref[...],
                         preferred_element_type=F32).astype(o_ref.dtype)


def _mm_res_kernel(a_ref, b_ref, r_ref, o_ref):
    acc = jnp.dot(a_ref[...], b_ref[...], preferred_element_type=F32)
    o_ref[...] = (acc + r_ref[...]).astype(o_ref.dtype)


def _mm_castb_kernel(a_ref, b_ref, o_ref):
    o_ref[...] = jnp.dot(a_ref[...], b_ref[...].astype(BF16),
                         preferred_element_type=F32).astype(o_ref.dtype)


def _matmul(a, b, *, tm, tn, out_dtype, residual=None, name="matmul"):
    m, k = a.shape
    _, n = b.shape
    in_specs = [pl.BlockSpec((tm, k), lambda i, j: (i, 0)),
                pl.BlockSpec((k, tn), lambda i, j: (0, j))]
    args = [a, b]
    kern = _mm_kernel if b.dtype == BF16 else _mm_castb_kernel
    if residual is not None:
        assert b.dtype == BF16
        in_specs.append(pl.BlockSpec((tm, tn), lambda i, j: (i, j)))
        args.append(residual)
        kern = _mm_res_kernel
    return pl.pallas_call(
        kern,
        grid=(m // tm, n // tn),
        in_specs=in_specs,
        out_specs=pl.BlockSpec((tm, tn), lambda i, j: (i, j)),
        out_shape=jax.ShapeDtypeStruct((m, n), out_dtype),
        compiler_params=_cparams("arbitrary", "arbitrary"),
        name=name,
    )(*args)


def _mm_res_norm_kernel(a_ref, b_ref, r_ref, g_ref, o_ref, xg_ref, ssq_ref):
    x = jnp.dot(a_ref[...], b_ref[...], preferred_element_type=F32) + r_ref[...]
    o_ref[...] = x
    xg_ref[...] = (x * g_ref[...]).astype(xg_ref.dtype)
    part = jnp.broadcast_to(jnp.sum(x * x, axis=-1, keepdims=True), ssq_ref.shape)

    @pl.when(pl.program_id(1) == 0)
    def _():
        ssq_ref[...] = part

    @pl.when(pl.program_id(1) > 0)
    def _():
        ssq_ref[...] += part


def _matmul_res_norm(a, b, residual, gain, *, tm, tn, name):
    m, k = a.shape
    _, n = b.shape
    return pl.pallas_call(
        _mm_res_norm_kernel,
        grid=(m // tm, n // tn),
        in_specs=[pl.BlockSpec((tm, k), lambda i, j: (i, 0)),
                  pl.BlockSpec((k, tn), lambda i, j: (0, j)),
                  pl.BlockSpec((tm, tn), lambda i, j: (i, j)),
                  pl.BlockSpec((1, tn), lambda i, j: (0, j))],
        out_specs=[pl.BlockSpec((tm, tn), lambda i, j: (i, j)),
                   pl.BlockSpec((tm, tn), lambda i, j: (i, j)),
                   pl.BlockSpec((tm, LANES), lambda i, j: (i, 0))],
        out_shape=[jax.ShapeDtypeStruct((m, n), F32),
                   jax.ShapeDtypeStruct((m, n), BF16),
                   jax.ShapeDtypeStruct((m, LANES), F32)],
        compiler_params=_cparams("arbitrary", "arbitrary"),
        name=name,
    )(a, b, residual, gain.reshape(1, n).astype(F32))


def _row_scale(ssq, width):
    return lax.rsqrt(ssq * (1.0 / width) + EPS)


def _cmul(ar, ai, br, bi):
    return ar * br - ai * bi, ar * bi + ai * br


def _shift_rows_down(x, k):
    rows = lax.broadcasted_iota(jnp.int32, x.shape, 0)
    return jnp.where(rows >= k, pltpu.roll(x, k, 0), 0.0)


def _ssm_kernel(u_ref, perm_ref, unperm_ref, wb_ref, wc_ref, cp_ref, cpb_ref, d_ref, z_ref,
                s_all_ref, carry_ref, *, units_per_step, steps_per_seq):
    sl = STATE_LANES
    re = pl.ds(0, sl)
    im = pl.ds(sl, sl)

    @pl.when(pl.program_id(1) % steps_per_seq == 0)
    def _():
        carry_ref[...] = jnp.zeros_like(carry_ref)

    def rows_of(i):
        return pl.ds(i * SUBLANES, SUBLANES)

    def const(k):
        rows = pl.ds(k * SUBLANES, SUBLANES)
        return cp_ref[0, rows, re], cp_ref[0, rows, im]

    base = SCAN_SUB

    a_re, a_im = const(0)
    c_re = carry_ref[:, re]
    c_im = carry_ref[:, im]
    def project_in(un):
        u = jnp.dot(perm_ref[...], u_ref[pl.ds(un * SCAN_UNIT, SCAN_UNIT), :],
                    preferred_element_type=F32).astype(BF16)
        s_all_ref[un] = jnp.dot(u, wb_ref[0], preferred_element_type=F32)
        return u

    u_next = project_in(0)
    for un in range(units_per_step):
        s_ref = s_all_ref.at[un]
        u = u_next
        if un + 1 < units_per_step:
            u_next = project_in(un + 1)

        h_re = jnp.zeros((SUBLANES, sl), F32)
        h_im = jnp.zeros((SUBLANES, sl), F32)
        for i in range(SCAN_SUB):
            p_re, p_im = _cmul(a_re, a_im, h_re, h_im)
            h_re = p_re + s_ref[rows_of(i), re]
            h_im = p_im + s_ref[rows_of(i), im]
            s_ref[rows_of(i), re] = h_re
            s_ref[rows_of(i), im] = h_im

        f_re, f_im = h_re, h_im
        for lvl, k in enumerate((1, 2, 4)):
            q_re, q_im = _cmul(*const(base + lvl),
                               _shift_rows_down(f_re, k), _shift_rows_down(f_im, k))
            f_re = f_re + q_re
            f_im = f_im + q_im
        in_re, in_im = _cmul(*const(base + 4), c_re, c_im)
        in_re = in_re + _shift_rows_down(f_re, 1)
        in_im = in_im + _shift_rows_down(f_im, 1)
        n_re, n_im = _cmul(*const(base + 3), c_re, c_im)
        last = SUBLANES - 1
        c_re = n_re + jnp.broadcast_to(f_re[last:last + 1], (SUBLANES, sl))
        c_im = n_im + jnp.broadcast_to(f_im[last:last + 1], (SUBLANES, sl))

        cin_re = jnp.concatenate([in_re, in_re], axis=0).astype(BF16)
        cin_im = jnp.concatenate([in_im, in_im], axis=0).astype(BF16)
        parts = []
        for i in range(0, SCAN_SUB, 2):
            rows = pl.ds(i * SUBLANES, 2 * SUBLANES)
            p_re, p_im = _cmul(cpb_ref[0, rows, re], cpb_ref[0, rows, im], cin_re, cin_im)
            parts.append(s_ref[rows, :].astype(BF16) + jnp.concatenate([p_re, p_im], axis=1))
        y = jnp.dot(jnp.concatenate(parts, axis=0), wc_ref[0], preferred_element_type=F32)
        y = y + d_ref[0] * u.astype(F32)
        z = jax.nn.gelu(y).astype(BF16)
        z_ref[pl.ds(un * SCAN_UNIT, SCAN_UNIT), :] = jnp.dot(
            unperm_ref[...], z, preferred_element_type=F32).astype(z_ref.dtype)
    carry_ref[:, re] = c_re
    carry_ref[:, im] = c_im


def _ssm(proj, wb, wc, cp, d, *, seq_len, tt):
    m = proj.shape[0]
    nb = wb.shape[0]
    kern = functools.partial(_ssm_kernel, units_per_step=tt // SCAN_UNIT,
                             steps_per_seq=seq_len // tt)
    r = jnp.arange(SCAN_UNIT)
    src = (r % SUBLANES) * SCAN_SUB + r // SUBLANES
    perm = (src[:, None] == r[None, :]).astype(BF16)
    return pl.pallas_call(
        kern,
        grid=(nb, m // tt),
        in_specs=[pl.BlockSpec((tt, LANES), lambda b, t: (t, b)),
                  pl.BlockSpec((SCAN_UNIT, SCAN_UNIT), lambda b, t: (0, 0)),
                  pl.BlockSpec((SCAN_UNIT, SCAN_UNIT), lambda b, t: (0, 0)),
                  pl.BlockSpec((1, LANES, 2 * STATE_LANES), lambda b, t: (b, 0, 0)),
                  pl.BlockSpec((1, 2 * STATE_LANES, LANES), lambda b, t: (b, 0, 0)),
                  pl.BlockSpec((1,) + cp.shape[1:], lambda b, t: (b, 0, 0)),
                  pl.BlockSpec((1, SCAN_UNIT, 2 * STATE_LANES), lambda b, t: (b, 0, 0)),
                  pl.BlockSpec((1, 1, LANES), lambda b, t: (b, 0, 0))],
        out_specs=pl.BlockSpec((tt, LANES), lambda b, t: (t, b)),
        out_shape=jax.ShapeDtypeStruct((m, nb * LANES), BF16),
        scratch_shapes=[pltpu.VMEM((tt // SCAN_UNIT, SCAN_UNIT, 2 * STATE_LANES), F32),
                        pltpu.VMEM((SUBLANES, 2 * STATE_LANES), F32)],
        compiler_params=_cparams("arbitrary", "arbitrary"),
        name="s5_scan",
    )(proj, perm, perm.T, wb, wc, cp, cp[:, :SCAN_UNIT].astype(BF16), d)


def _ssm_params(lam_re, lam_im, log_step, b_re, b_im, c_re, c_im, d):
    g, p = lam_re.shape
    h = b_re.shape[-1]
    nb = g // GROUPS_PER_BLOCK
    lr = lam_re.astype(F32)
    li = lam_im.astype(F32)
    step = jnp.exp(log_step.astype(F32))[:, None]
    mag = jnp.exp(lr * step)
    ang = li * step
    abar_re, abar_im = mag * jnp.cos(ang), mag * jnp.sin(ang)
    den = lr * lr + li * li
    nr, ni = abar_re - 1.0, abar_im
    coef_re = (nr * lr + ni * li) / den
    coef_im = (ni * lr - nr * li) / den
    br_f, bi_f = b_re.astype(F32), b_im.astype(F32)
    bbar_re = coef_re[..., None] * br_f - coef_im[..., None] * bi_f
    bbar_im = coef_re[..., None] * bi_f + coef_im[..., None] * br_f
    eye = jnp.eye(GROUPS_PER_BLOCK, dtype=F32)

    def in_block(x):
        x = x.reshape(nb, GROUPS_PER_BLOCK, p, h).transpose(0, 1, 3, 2)
        x = jnp.einsum('lghp,gk->lghkp', x, eye)
        return x.reshape(nb, LANES, STATE_LANES)

    def out_block(x):
        x = x.reshape(nb, GROUPS_PER_BLOCK, h, p).transpose(0, 1, 3, 2)
        x = jnp.einsum('lgph,gk->lgpkh', x, eye)
        return x.reshape(nb, STATE_LANES, LANES)

    wb = jnp.concatenate([in_block(bbar_re), in_block(bbar_im)], axis=-1).astype(BF16)
    wc = jnp.concatenate([out_block(c_re.astype(F32)), out_block(-c_im.astype(F32))],
                         axis=1).astype(BF16)

    a_re = abar_re.reshape(nb, STATE_LANES)
    a_im = abar_im.reshape(nb, STATE_LANES)
    pows = [(a_re, a_im)]
    for _ in range(SCAN_SUB - 1):
        pows.append(_cmul(a_re, a_im, *pows[-1]))
    sub = pows[-1]
    one = (jnp.ones_like(a_re), jnp.zeros_like(a_im))
    per_sublane = [one]
    for _ in range(SUBLANES - 1):
        per_sublane.append(_cmul(*sub, *per_sublane[-1]))
    sub2 = _cmul(*sub, *sub)
    sub4 = _cmul(*sub2, *sub2)
    sub8 = _cmul(*sub4, *sub4)

    def table(pairs):
        return jnp.stack([jnp.concatenate(c, axis=-1) for c in pairs], axis=1)

    repeated = jnp.repeat(table(pows + [sub, sub2, sub4, sub8]), SUBLANES, axis=1)
    cp = jnp.concatenate([repeated, table(per_sublane)], axis=1)
    return wb, wc, cp, d.astype(F32).reshape(nb, 1, LANES)


def _mix_kernel(z_ref, gw_ref, gb_ref, ng_s_ref, pb_ref, pc_ref, pv_ref, hc_ref, hv_ref,
                cw_ref, ng_c_ref, o_ref, w_ref, *, tm, width, tiles_per_seq):
    z = z_ref[...]
    lin = jnp.dot(z, gw_ref[...], preferred_element_type=F32) + gb_ref[...]
    y = z.astype(F32) * jax.nn.sigmoid(lin)
    o_ref[:, pl.ds(0, width)] = (y * _rms_scale(y) * ng_s_ref[...]).astype(o_ref.dtype)

    cv = pc_ref[...].astype(F32) * pv_ref[...].astype(F32)
    halo = hc_ref[...].astype(F32) * hv_ref[...].astype(F32)
    first = pl.program_id(0) % tiles_per_seq == 0
    w_ref[pl.ds(0, SUBLANES), :] = jnp.where(first, 0.0, halo)
    w_ref[pl.ds(SUBLANES, tm), :] = cv
    conv = (cw_ref[0:1, :] * w_ref[pl.ds(SUBLANES - 2, tm), :]
            + cw_ref[1:2, :] * w_ref[pl.ds(SUBLANES - 1, tm), :]
            + cw_ref[2:3, :] * cv)
    yc = pb_ref[...].astype(F32) * conv
    o_ref[:, pl.ds(width, width)] = (yc * _rms_scale(yc) * ng_c_ref[...]).astype(o_ref.dtype)


def _mix(z, proj, glu_w, glu_b, ng_s, conv_w, ng_c, *, seq_len, tm):
    m, width = z.shape
    cb = width // width
    halo_blocks = tm // SUBLANES
    kern = functools.partial(_mix_kernel, tm=tm, width=width, tiles_per_seq=seq_len // tm)
    row = lambda a: a.reshape(1, width).astype(F32)

    def halo_map(col):
        return lambda i: (jnp.maximum(i * halo_blocks - 1, 0), col)

    return pl.pallas_call(
        kern,
        grid=(m // tm,),
        in_specs=[pl.BlockSpec((tm, width), lambda i: (i, 0)),
                  pl.BlockSpec((width, width), lambda i: (0, 0)),
                  pl.BlockSpec((1, width), lambda i: (0, 0)),
                  pl.BlockSpec((1, width), lambda i: (0, 0)),
                  pl.BlockSpec((tm, width), lambda i: (i, 1 * cb)),
                  pl.BlockSpec((tm, width), lambda i: (i, 2 * cb)),
                  pl.BlockSpec((tm, width), lambda i: (i, 3 * cb)),
                  pl.BlockSpec((SUBLANES, width), halo_map(2 * cb)),
                  pl.BlockSpec((SUBLANES, width), halo_map(3 * cb)),
                  pl.BlockSpec((3, width), lambda i: (0, 0)),
                  pl.BlockSpec((1, width), lambda i: (0, 0))],
        out_specs=pl.BlockSpec((tm, 2 * width), lambda i: (i, 0)),
        out_shape=jax.ShapeDtypeStruct((m, 2 * width), BF16),
        scratch_shapes=[pltpu.VMEM((tm + SUBLANES, width), F32)],
        compiler_params=_cparams("arbitrary"),
        name="mixer_tail",
    )(z, glu_w, row(glu_b), row(ng_s), proj, proj, proj, proj, proj,
      conv_w.astype(F32), row(ng_c))


def _xattn_kernel(hq_ref, ssq_ref, wq_ref, k_ref, v_ref, o_ref, *, scale):
    q = jnp.dot(hq_ref[...], wq_ref[...], preferred_element_type=F32)
    s = lax.dot_general(q.astype(BF16), k_ref[...], (((1,), (1,)), ((), ())),
                        preferred_element_type=F32)
    rs = _row_scale(ssq_ref[...], hq_ref.shape[1]) * scale
    s = s * jnp.concatenate([rs] * (s.shape[1] // LANES), axis=1)
    s = s - jnp.max(s, axis=-1, keepdims=True)
    e = jnp.exp(s)
    p = e / jnp.sum(e, axis=-1, keepdims=True)
    o_ref[...] = jnp.dot(p.astype(BF16), v_ref[...],
                         preferred_element_type=F32).astype(o_ref.dtype)


def _xattn(hq, ssq, wq, k, v, *, seq_len, n_mem, tm):
    m, d = hq.shape
    hd = d // XATTN_HEADS
    tiles_per_seq = seq_len // tm
    kern = functools.partial(_xattn_kernel, scale=float(hd) ** -0.5)
    return pl.pallas_call(
        kern,
        grid=(m // tm, XATTN_HEADS),
        in_specs=[pl.BlockSpec((tm, d), lambda i, h: (i, 0)),
                  pl.BlockSpec((tm, LANES), lambda i, h: (i, 0)),
                  pl.BlockSpec((d, hd), lambda i, h: (0, h)),
                  pl.BlockSpec((n_mem, hd), lambda i, h: (i // tiles_per_seq, h)),
                  pl.BlockSpec((n_mem, hd), lambda i, h: (i // tiles_per_seq, h))],
        out_specs=pl.BlockSpec((tm, hd), lambda i, h: (i, h)),
        out_shape=jax.ShapeDtypeStruct((m, d), BF16),
        compiler_params=_cparams("arbitrary", "arbitrary"),
        name="xattn",
    )(hq, ssq, wq, k, v)


def _ffn_up_kernel(h_ref, ssq_ref, wa_ref, wg_ref, cw_ref, cb_ref, o_ref,
                   wab_ref, wgb_ref, a0_ref, g0_ref, a1_ref, g1_ref, stage_ref, halo_ref,
                   *, tm, n_i, n_steps, tiles_per_seq, n_phase):
    t = pl.program_id(0)
    i = t % n_i

    @pl.when(jnp.logical_and(i == 0, t < n_steps - 1))
    def _():
        wab_ref[...] = wa_ref[...].astype(BF16)
        wgb_ref[...] = wg_ref[...].astype(BF16)

    mc = tm // n_phase
    rc = 16
    n_cols = o_ref.shape[1] // LANES
    tail_seq_start = (jnp.maximum(t - 1, 0) % n_i) % tiles_per_seq == 0

    def tail_piece(a_prev, g_prev, r0, col, above, zero):
        lanes = pl.ds(col * LANES, LANES)
        rows = pl.ds(r0, rc)
        a = a_prev[rows, lanes]
        if zero is not None:
            a = lax.bitcast_convert_type(lax.bitcast_convert_type(a, jnp.uint32) + zero, F32)
        rs = _row_scale(ssq_ref[rows, :], h_ref.shape[1])
        a = a * rs
        ext = jnp.concatenate([above, a], axis=0)
        conv = (cw_ref[0:1, lanes] * ext[SUBLANES - 2:SUBLANES - 2 + rc]
                + cw_ref[1:2, lanes] * ext[SUBLANES - 1:SUBLANES - 1 + rc]
                + cw_ref[2:3, lanes] * a) + cb_ref[:, lanes]
        out = jax.nn.silu(conv) * (g_prev[rows, lanes] * rs)
        stage_ref[rows, lanes] = out.astype(stage_ref.dtype)
        bits = lax.bitcast_convert_type(out, jnp.uint32)
        return a[rc - SUBLANES:], (bits >> 16) >> 16

    def step(cur, prev, dots, tail):
        kc = h_ref.shape[1] // n_phase
        a_acc = g_acc = None
        zeros = [None] * n_cols
        if tail:
            above = [jnp.where(tail_seq_start, 0.0, halo_ref[col]) for col in range(n_cols)]
        for p in range(n_phase):
            if dots:
                hb = h_ref[:, pl.ds(p * kc, kc)]
                pa = jnp.dot(hb, wab_ref[pl.ds(p * kc, kc), :], preferred_element_type=F32)
                pg = jnp.dot(hb, wgb_ref[pl.ds(p * kc, kc), :], preferred_element_type=F32)
                a_acc = pa if a_acc is None else a_acc + pa
                g_acc = pg if g_acc is None else g_acc + pg
            if tail:
                for r0 in range(p * mc, (p + 1) * mc, rc):
                    for col in range(n_cols):
                        above[col], zeros[col] = tail_piece(prev[0], prev[1], r0, col,
                                                            above[col], zeros[col])
        if tail:
            for col in range(n_cols):
                halo_ref[col] = above[col]
            o_ref[...] = stage_ref[...]
        if dots:
            cur[0][...] = a_acc
            cur[1][...] = g_acc

    slots = ((a0_ref, g0_ref), (a1_ref, g1_ref))
    middle = jnp.logical_and(t > 0, t < n_steps - 1)

    @pl.when(t == 0)
    def _():
        halo_ref[...] = jnp.zeros_like(halo_ref)
        step(slots[0], None, dots=True, tail=False)

    for parity in (0, 1):
        @pl.when(jnp.logical_and(middle, t % 2 == parity))
        def _():
            step(slots[parity], slots[1 - parity], dots=True, tail=True)

    @pl.when(t == n_steps - 1)
    def _():
        step(None, slots[(n_steps - 2) % 2], dots=False, tail=True)


def _ffn_up(h, ssq, w_up, conv_w, conv_b, *, seq_len, tm, tf):
    m, d = h.shape
    d_ff = conv_w.shape[-1]
    nf = d_ff // tf
    n_i = m // tm
    n_steps = nf * n_i + 1
    kern = functools.partial(_ffn_up_kernel, tm=tm, n_i=n_i, n_steps=n_steps, n_phase=4,
                             tiles_per_seq=seq_len // tm)

    def tile_j(t):
        return jnp.minimum(t, n_steps - 2) // n_i

    def tail_i(t):
        return jnp.maximum(t - 1, 0) % n_i

    def tail_j(t):
        return jnp.maximum(t - 1, 0) // n_i

    return pl.pallas_call(
        kern,
        grid=(n_steps,),
        in_specs=[pl.BlockSpec((tm, d), lambda t: (jnp.minimum(t, n_steps - 2) % n_i, 0)),
                  pl.BlockSpec((tm, LANES), lambda t: (tail_i(t), 0)),
                  pl.BlockSpec((d, tf), lambda t: (0, tile_j(t))),
                  pl.BlockSpec((d, tf), lambda t: (0, tile_j(t) + nf)),
                  pl.BlockSpec((3, tf), lambda t: (0, tail_j(t))),
                  pl.BlockSpec((1, tf), lambda t: (0, tail_j(t)))],
        out_specs=pl.BlockSpec((tm, tf), lambda t: (tail_i(t), tail_j(t))),
        out_shape=jax.ShapeDtypeStruct((m, d_ff), BF16),
        scratch_shapes=[pltpu.VMEM((d, tf), BF16), pltpu.VMEM((d, tf), BF16),
                        pltpu.VMEM((tm, tf), F32), pltpu.VMEM((tm, tf), F32),
                        pltpu.VMEM((tm, tf), F32), pltpu.VMEM((tm, tf), F32),
                        pltpu.VMEM((tm, tf), BF16),
                        pltpu.VMEM((tf // LANES, SUBLANES, LANES), F32)],
        compiler_params=_cparams("arbitrary"),
        name="ffn_up",
    )(h, ssq, w_up, w_up, conv_w.astype(F32), conv_b.reshape(1, d_ff).astype(F32))


def kernel(x, mem, norm_mix_g, w_in, ssm_lambda_re, ssm_lambda_im, ssm_log_step, ssm_b_re, ssm_b_im, ssm_c_re, ssm_c_im, ssm_d, ssm_glu_w, ssm_glu_b, conv_w, out_norm_ssm_g, out_norm_conv_g, w_out, norm_xattn_g, norm_mem_g, xattn_wq, xattn_wk, xattn_wv, xattn_wo, norm_ffn_g, ffn_w_up, ffn_conv_w, ffn_conv_b, ffn_w_down, norm_final_g):
    bsz, seq_len, d_model = x.shape
    n_mem = mem.shape[1]
    depth = w_in.shape[0]
    m = bsz * seq_len
    xs = x.reshape(m, d_model)
    mems = mem.reshape(bsz * n_mem, d_model)
    for l in range(depth):
        h = _rmsnorm(xs, norm_mix_g[l], BF16, tm=512)
        proj = _matmul(h, w_in[l].astype(BF16), tm=1024, tn=1024, out_dtype=BF16, name="proj_in")
        wb, wc, cp, d = _ssm_params(ssm_lambda_re[l], ssm_lambda_im[l], ssm_log_step[l],
                                    ssm_b_re[l], ssm_b_im[l], ssm_c_re[l], ssm_c_im[l], ssm_d[l])
        z = _ssm(proj, wb, wc, cp, d, seq_len=seq_len, tt=4096)
        mixed = _mix(z, proj, ssm_glu_w[l].astype(BF16), ssm_glu_b[l], out_norm_ssm_g[l],
                     conv_w[l], out_norm_conv_g[l], seq_len=seq_len, tm=256)
        xs, hq, ssq = _matmul_res_norm(mixed, w_out[l].astype(BF16), xs, norm_xattn_g[l],
                                       tm=1024, tn=512, name="proj_out")
        hm = _rmsnorm(mems, norm_mem_g[l], BF16, tm=512)
        k = _matmul(hm, xattn_wk[l], tm=1024, tn=512, out_dtype=BF16, name="proj_k")
        v = _matmul(hm, xattn_wv[l], tm=1024, tn=512, out_dtype=BF16, name="proj_v")
        o = _xattn(hq, ssq, xattn_wq[l].astype(BF16), k, v,
                   seq_len=seq_len, n_mem=n_mem, tm=1024)
        xs, h, ssq = _matmul_res_norm(o, xattn_wo[l].astype(BF16), xs, norm_ffn_g[l],
                                      tm=1024, tn=512, name="proj_o")
        act = _ffn_up(h, ssq, ffn_w_up[l], ffn_conv_w[l], ffn_conv_b[l],
                      seq_len=seq_len, tm=1024, tf=256)
        xs = _matmul(act, ffn_w_down[l].astype(BF16), tm=512, tn=512, out_dtype=F32,
                     residual=xs, name="ffn_down")
    out = _rmsnorm(xs, norm_final_g, x.dtype, tm=512)
    return out.reshape(bsz, seq_len, d_model)
```

```python
import functools

import jax
import jax.numpy as jnp
from jax import lax
from jax.experimental import pallas as pl
from jax.experimental.pallas import tpu as pltpu

F32 = jnp.float32
BF16 = jnp.bfloat16

EPS = 1e-6
SSM_GROUP = 16
SSM_STATE = 64
XATTN_HEADS = 4

LANES = 128
SUBLANES = 8
BF16_ROWS = 2 * SUBLANES
V7X_VMEM_LIMIT_BYTES = 56 * 1024 * 1024

GROUPS_PER_BLOCK = LANES // SSM_GROUP
STATE_LANES = GROUPS_PER_BLOCK * SSM_STATE
SCAN_SUB = 32
SCAN_UNIT = SUBLANES * SCAN_SUB

NORM_ROWS = 512
PROJ_IN_TILE = (1024, 1024)
PROJ_RES_TILE = (1024, 512)
KV_TILE = (1024, 512)
XATTN_ROWS = 1024
MIX_ROWS = 256
SSM_ROWS = 4096
FFN_UP_TILE = (1024, 256)
FFN_DOWN_TILE = (512, 512)
OVERLAP_PHASES = 4
MIX_CHAINS = 2


def _cparams(*sem):
    return pltpu.CompilerParams(dimension_semantics=sem,
                                vmem_limit_bytes=V7X_VMEM_LIMIT_BYTES)


def _rms_scale(xf):
    return lax.rsqrt(jnp.mean(xf * xf, axis=-1, keepdims=True) + EPS)


def _row_scale(ssq, width):
    return lax.rsqrt(ssq * (1.0 / width) + EPS)


def _chain_zero(x):
    return (lax.bitcast_convert_type(x, jnp.uint32) >> 16) >> 16


def _after(x, zero):
    if zero is None:
        return x
    return lax.bitcast_convert_type(lax.bitcast_convert_type(x, jnp.uint32) + zero, x.dtype)


def _rmsnorm_kernel(x_ref, g_ref, o_ref):
    xf = x_ref[...].astype(F32)
    o_ref[...] = (xf * _rms_scale(xf) * g_ref[...]).astype(o_ref.dtype)


def _rmsnorm(x, g, out_dtype):
    m, d = x.shape
    tm = NORM_ROWS
    assert m % tm == 0
    return pl.pallas_call(
        _rmsnorm_kernel,
        grid=(m // tm,),
        in_specs=[pl.BlockSpec((tm, d), lambda i: (i, 0)),
                  pl.BlockSpec((1, d), lambda i: (0, 0))],
        out_specs=pl.BlockSpec((tm, d), lambda i: (i, 0)),
        out_shape=jax.ShapeDtypeStruct((m, d), out_dtype),
        compiler_params=_cparams("arbitrary"),
        name="rmsnorm",
    )(x, g.reshape(1, d).astype(F32))


def _mm_kernel(a_ref, b_ref, o_ref):
    o_ref[...] = jnp.dot(a_ref[...], b_ref[...],
                         preferred_element_type=F32).astype(o_ref.dtype)


def _mm_res_kernel(a_ref, b_ref, r_ref, o_ref):
    acc = jnp.dot(a_ref[...], b_ref[...], preferred_element_type=F32)
    o_ref[...] = (acc + r_ref[...]).astype(o_ref.dtype)


def _mm_castb_kernel(a_ref, b_ref, o_ref):
    o_ref[...] = jnp.dot(a_ref[...], b_ref[...].astype(BF16),
                         preferred_element_type=F32).astype(o_ref.dtype)


def _matmul(a, b, *, tile, out_dtype, residual=None, name):
    m, k = a.shape
    _, n = b.shape
    tm, tn = min(tile[0], m), tile[1]
    assert m % tm == 0 and n % tn == 0
    in_specs = [pl.BlockSpec((tm, k), lambda i, j: (i, 0)),
                pl.BlockSpec((k, tn), lambda i, j: (0, j))]
    args = [a, b]
    kern = _mm_kernel if b.dtype == BF16 else _mm_castb_kernel
    if residual is not None:
        assert b.dtype == BF16
        in_specs.append(pl.BlockSpec((tm, tn), lambda i, j: (i, j)))
        args.append(residual)
        kern = _mm_res_kernel
    return pl.pallas_call(
        kern,
        grid=(m // tm, n // tn),
        in_specs=in_specs,
        out_specs=pl.BlockSpec((tm, tn), lambda i, j: (i, j)),
        out_shape=jax.ShapeDtypeStruct((m, n), out_dtype),
        compiler_params=_cparams("arbitrary", "arbitrary"),
        name=name,
    )(*args)


def _mm_res_norm_kernel(a_ref, b_ref, r_ref, g_ref, o_ref, xg_ref, ssq_ref):
    x = jnp.dot(a_ref[...], b_ref[...], preferred_element_type=F32) + r_ref[...]
    o_ref[...] = x
    xg_ref[...] = (x * g_ref[...]).astype(xg_ref.dtype)
    part = jnp.broadcast_to(jnp.sum(x * x, axis=-1, keepdims=True), ssq_ref.shape)

    @pl.when(pl.program_id(1) == 0)
    def _():
        ssq_ref[...] = part

    @pl.when(pl.program_id(1) > 0)
    def _():
        ssq_ref[...] += part


def _matmul_res_norm(a, b, residual, gain, *, name):
    m, k = a.shape
    _, n = b.shape
    tm, tn = PROJ_RES_TILE
    assert m % tm == 0 and n % tn == 0
    return pl.pallas_call(
        _mm_res_norm_kernel,
        grid=(m // tm, n // tn),
        in_specs=[pl.BlockSpec((tm, k), lambda i, j: (i, 0)),
                  pl.BlockSpec((k, tn), lambda i, j: (0, j)),
                  pl.BlockSpec((tm, tn), lambda i, j: (i, j)),
                  pl.BlockSpec((1, tn), lambda i, j: (0, j))],
        out_specs=[pl.BlockSpec((tm, tn), lambda i, j: (i, j)),
                   pl.BlockSpec((tm, tn), lambda i, j: (i, j)),
                   pl.BlockSpec((tm, LANES), lambda i, j: (i, 0))],
        out_shape=[jax.ShapeDtypeStruct((m, n), F32),
                   jax.ShapeDtypeStruct((m, n), BF16),
                   jax.ShapeDtypeStruct((m, LANES), F32)],
        compiler_params=_cparams("arbitrary", "arbitrary"),
        name=name,
    )(a, b, residual, gain.reshape(1, n).astype(F32))


def _cmul(ar, ai, br, bi):
    return ar * br - ai * bi, ar * bi + ai * br


def _shift_rows_down(x, k):
    rows = lax.broadcasted_iota(jnp.int32, x.shape, 0)
    return jnp.where(rows >= k, pltpu.roll(x, k, 0), 0.0)


def _ssm_kernel(u_ref, perm_ref, unperm_ref, wb_ref, wc_ref, cp_ref, cpb_ref, d_ref, z_ref,
                s_all_ref, carry_ref, *, units_per_step, steps_per_seq):
    sl = STATE_LANES
    re = pl.ds(0, sl)
    im = pl.ds(sl, sl)

    @pl.when(pl.program_id(1) % steps_per_seq == 0)
    def _():
        carry_ref[...] = jnp.zeros_like(carry_ref)

    def rows_of(i):
        return pl.ds(i * SUBLANES, SUBLANES)

    def const(k):
        rows = pl.ds(k * SUBLANES, SUBLANES)
        return cp_ref[0, rows, re], cp_ref[0, rows, im]

    base = SCAN_SUB

    def project_in(un):
        u = jnp.dot(perm_ref[...], u_ref[pl.ds(un * SCAN_UNIT, SCAN_UNIT), :],
                    preferred_element_type=F32).astype(BF16)
        s_all_ref[un] = jnp.dot(u, wb_ref[0], preferred_element_type=F32)
        return u

    a_re, a_im = const(0)
    c_re = carry_ref[:, re]
    c_im = carry_ref[:, im]
    u_next = project_in(0)
    for un in range(units_per_step):
        s_ref = s_all_ref.at[un]
        u = u_next
        if un + 1 < units_per_step:
            u_next = project_in(un + 1)

        h_re = jnp.zeros((SUBLANES, sl), F32)
        h_im = jnp.zeros((SUBLANES, sl), F32)
        for i in range(SCAN_SUB):
            p_re, p_im = _cmul(a_re, a_im, h_re, h_im)
            h_re = p_re + s_ref[rows_of(i), re]
            h_im = p_im + s_ref[rows_of(i), im]
            s_ref[rows_of(i), re] = h_re
            s_ref[rows_of(i), im] = h_im

        f_re, f_im = h_re, h_im
        for lvl, k in enumerate((1, 2, 4)):
            q_re, q_im = _cmul(*const(base + lvl),
                               _shift_rows_down(f_re, k), _shift_rows_down(f_im, k))
            f_re = f_re + q_re
            f_im = f_im + q_im
        in_re, in_im = _cmul(*const(base + 4), c_re, c_im)
        in_re = in_re + _shift_rows_down(f_re, 1)
        in_im = in_im + _shift_rows_down(f_im, 1)
        n_re, n_im = _cmul(*const(base + 3), c_re, c_im)
        last = SUBLANES - 1
        c_re = n_re + jnp.broadcast_to(f_re[last:last + 1], (SUBLANES, sl))
        c_im = n_im + jnp.broadcast_to(f_im[last:last + 1], (SUBLANES, sl))

        cin_re = jnp.concatenate([in_re, in_re], axis=0).astype(BF16)
        cin_im = jnp.concatenate([in_im, in_im], axis=0).astype(BF16)
        parts = []
        for i in range(0, SCAN_SUB, 2):
            rows = pl.ds(i * SUBLANES, BF16_ROWS)
            p_re, p_im = _cmul(cpb_ref[0, rows, re], cpb_ref[0, rows, im], cin_re, cin_im)
            parts.append(s_ref[rows, :].astype(BF16) + jnp.concatenate([p_re, p_im], axis=1))
        y = jnp.dot(jnp.concatenate(parts, axis=0), wc_ref[0], preferred_element_type=F32)
        y = y + d_ref[0] * u.astype(F32)
        z = jax.nn.gelu(y).astype(BF16)
        z_ref[pl.ds(un * SCAN_UNIT, SCAN_UNIT), :] = jnp.dot(
            unperm_ref[...], z, preferred_element_type=F32).astype(z_ref.dtype)
    carry_ref[:, re] = c_re
    carry_ref[:, im] = c_im


def _ssm(proj, wb, wc, cp, d, *, seq_len):
    m = proj.shape[0]
    nb = wb.shape[0]
    tt = min(SSM_ROWS, seq_len)
    assert seq_len % tt == 0 and tt % SCAN_UNIT == 0
    kern = functools.partial(_ssm_kernel, units_per_step=tt // SCAN_UNIT,
                             steps_per_seq=seq_len // tt)
    r = jnp.arange(SCAN_UNIT)
    src = (r % SUBLANES) * SCAN_SUB + r // SUBLANES
    perm = (src[:, None] == r[None, :]).astype(BF16)
    return pl.pallas_call(
        kern,
        grid=(nb, m // tt),
        in_specs=[pl.BlockSpec((tt, LANES), lambda b, t: (t, b)),
                  pl.BlockSpec((SCAN_UNIT, SCAN_UNIT), lambda b, t: (0, 0)),
                  pl.BlockSpec((SCAN_UNIT, SCAN_UNIT), lambda b, t: (0, 0)),
                  pl.BlockSpec((1, LANES, 2 * STATE_LANES), lambda b, t: (b, 0, 0)),
                  pl.BlockSpec((1, 2 * STATE_LANES, LANES), lambda b, t: (b, 0, 0)),
                  pl.BlockSpec((1,) + cp.shape[1:], lambda b, t: (b, 0, 0)),
                  pl.BlockSpec((1, SCAN_UNIT, 2 * STATE_LANES), lambda b, t: (b, 0, 0)),
                  pl.BlockSpec((1, 1, LANES), lambda b, t: (b, 0, 0))],
        out_specs=pl.BlockSpec((tt, LANES), lambda b, t: (t, b)),
        out_shape=jax.ShapeDtypeStruct((m, nb * LANES), BF16),
        scratch_shapes=[pltpu.VMEM((tt // SCAN_UNIT, SCAN_UNIT, 2 * STATE_LANES), F32),
                        pltpu.VMEM((SUBLANES, 2 * STATE_LANES), F32)],
        compiler_params=_cparams("arbitrary", "arbitrary"),
        name="s5_scan",
    )(proj, perm, perm.T, wb, wc, cp, cp[:, :SCAN_UNIT].astype(BF16), d)


def _ssm_params(lam_re, lam_im, log_step, b_re, b_im, c_re, c_im, d):
    g, p = lam_re.shape
    h = b_re.shape[-1]
    assert (h, p) == (SSM_GROUP, SSM_STATE) and g % GROUPS_PER_BLOCK == 0
    nb = g // GROUPS_PER_BLOCK
    lr = lam_re.astype(F32)
    li = lam_im.astype(F32)
    step = jnp.exp(log_step.astype(F32))[:, None]
    mag = jnp.exp(lr * step)
    ang = li * step
    abar_re, abar_im = mag * jnp.cos(ang), mag * jnp.sin(ang)
    den = lr * lr + li * li
    nr, ni = abar_re - 1.0, abar_im
    coef_re = (nr * lr + ni * li) / den
    coef_im = (ni * lr - nr * li) / den
    br_f, bi_f = b_re.astype(F32), b_im.astype(F32)
    bbar_re = coef_re[..., None] * br_f - coef_im[..., None] * bi_f
    bbar_im = coef_re[..., None] * bi_f + coef_im[..., None] * br_f
    eye = jnp.eye(GROUPS_PER_BLOCK, dtype=F32)

    def in_block(x):
        x = x.reshape(nb, GROUPS_PER_BLOCK, p, h).transpose(0, 1, 3, 2)
        x = jnp.einsum('lghp,gk->lghkp', x, eye)
        return x.reshape(nb, LANES, STATE_LANES)

    def out_block(x):
        x = x.reshape(nb, GROUPS_PER_BLOCK, h, p).transpose(0, 1, 3, 2)
        x = jnp.einsum('lgph,gk->lgpkh', x, eye)
        return x.reshape(nb, STATE_LANES, LANES)

    wb = jnp.concatenate([in_block(bbar_re), in_block(bbar_im)], axis=-1).astype(BF16)
    wc = jnp.concatenate([out_block(c_re.astype(F32)), out_block(-c_im.astype(F32))],
                         axis=1).astype(BF16)

    a_re = abar_re.reshape(nb, STATE_LANES)
    a_im = abar_im.reshape(nb, STATE_LANES)
    pows = [(a_re, a_im)]
    for _ in range(SCAN_SUB - 1):
        pows.append(_cmul(a_re, a_im, *pows[-1]))
    sub = pows[-1]
    one = (jnp.ones_like(a_re), jnp.zeros_like(a_im))
    per_sublane = [one]
    for _ in range(SUBLANES - 1):
        per_sublane.append(_cmul(*sub, *per_sublane[-1]))
    sub2 = _cmul(*sub, *sub)
    sub4 = _cmul(*sub2, *sub2)
    sub8 = _cmul(*sub4, *sub4)

    def table(pairs):
        return jnp.stack([jnp.concatenate(c, axis=-1) for c in pairs], axis=1)

    repeated = jnp.repeat(table(pows + [sub, sub2, sub4, sub8]), SUBLANES, axis=1)
    cp = jnp.concatenate([repeated, table(per_sublane)], axis=1)
    return wb, wc, cp, d.astype(F32).reshape(nb, 1, LANES)


def _mix_kernel(z_ref, gw_ref, gb_ref, ng_s_ref, pb_ref, pc_ref, pv_ref, hc_ref, hv_ref,
                cw_ref, ng_c_ref, o_ref, lin_ref, yc_ref, sq_ref,
                *, tm, width, tiles_per_seq, n_chain):
    z = z_ref[...]
    lin_ref[...] = jnp.dot(z, gw_ref[...], preferred_element_type=F32)

    n_cols = width // LANES
    cols_per_chain = n_cols // n_chain
    rc = BF16_ROWS
    first = pl.program_id(0) % tiles_per_seq == 0
    above = []
    for col in range(n_cols):
        lanes = pl.ds(col * LANES, LANES)
        halo = hc_ref[:, lanes].astype(F32) * hv_ref[:, lanes].astype(F32)
        above.append(jnp.where(first, 0.0, halo))
    zeros = [None] * n_chain
    for r0 in range(0, tm, rc):
        rows = pl.ds(r0, rc)
        sq = [None] * n_chain
        for k in range(cols_per_chain):
            for c in range(n_chain):
                col = c * cols_per_chain + k
                lanes = pl.ds(col * LANES, LANES)
                cv = _after(pc_ref[rows, lanes].astype(F32), zeros[c]) \
                    * pv_ref[rows, lanes].astype(F32)
                ext = jnp.concatenate([above[col], cv], axis=0)
                conv = (cw_ref[0:1, lanes] * ext[SUBLANES - 2:SUBLANES - 2 + rc]
                        + cw_ref[1:2, lanes] * ext[SUBLANES - 1:SUBLANES - 1 + rc]
                        + cw_ref[2:3, lanes] * cv)
                yc = pb_ref[rows, lanes].astype(F32) * conv
                yc_ref[rows, lanes] = yc
                sq[c] = yc * yc if sq[c] is None else sq[c] + yc * yc
                above[col] = cv[rc - SUBLANES:]
                zeros[c] = _chain_zero(yc)
        total = sq[0]
        for part in sq[1:]:
            total = total + part
        sq_ref[rows, :] = total

    y = z.astype(F32) * jax.nn.sigmoid(lin_ref[...] + gb_ref[...])
    o_ref[:, pl.ds(0, width)] = (y * _rms_scale(y) * ng_s_ref[...]).astype(o_ref.dtype)

    ms = jnp.sum(sq_ref[...], axis=-1, keepdims=True) * (1.0 / width)
    o_ref[:, pl.ds(width, width)] = (yc_ref[...] * lax.rsqrt(ms + EPS)
                                     * ng_c_ref[...]).astype(o_ref.dtype)


def _mix(z, proj, glu_w, glu_b, ng_s, conv_w, ng_c, *, seq_len):
    m, width = z.shape
    tm = MIX_ROWS
    assert proj.shape[1] == 4 * width and seq_len % tm == 0
    assert (width // LANES) % MIX_CHAINS == 0
    halo_blocks = tm // SUBLANES
    kern = functools.partial(_mix_kernel, tm=tm, width=width, tiles_per_seq=seq_len // tm,
                             n_chain=MIX_CHAINS)
    row = lambda a: a.reshape(1, width).astype(F32)

    def halo_map(col):
        return lambda i: (jnp.maximum(i * halo_blocks - 1, 0), col)

    return pl.pallas_call(
        kern,
        grid=(m // tm,),
        in_specs=[pl.BlockSpec((tm, width), lambda i: (i, 0)),
                  pl.BlockSpec((width, width), lambda i: (0, 0)),
                  pl.BlockSpec((1, width), lambda i: (0, 0)),
                  pl.BlockSpec((1, width), lambda i: (0, 0)),
                  pl.BlockSpec((tm, width), lambda i: (i, 1)),
                  pl.BlockSpec((tm, width), lambda i: (i, 2)),
                  pl.BlockSpec((tm, width), lambda i: (i, 3)),
                  pl.BlockSpec((SUBLANES, width), halo_map(2)),
                  pl.BlockSpec((SUBLANES, width), halo_map(3)),
                  pl.BlockSpec((3, width), lambda i: (0, 0)),
                  pl.BlockSpec((1, width), lambda i: (0, 0))],
        out_specs=pl.BlockSpec((tm, 2 * width), lambda i: (i, 0)),
        out_shape=jax.ShapeDtypeStruct((m, 2 * width), BF16),
        scratch_shapes=[pltpu.VMEM((tm, width), F32), pltpu.VMEM((tm, width), F32),
                        pltpu.VMEM((tm, LANES), F32)],
        compiler_params=_cparams("arbitrary"),
        name="mixer_tail",
    )(z, glu_w, row(glu_b), row(ng_s), proj, proj, proj, proj, proj,
      conv_w.astype(F32), row(ng_c))


def _xattn_kernel(hq_ref, ssq_ref, wq_ref, k_ref, v_ref, o_ref, *, scale):
    q = jnp.dot(hq_ref[...], wq_ref[...], preferred_element_type=F32)
    s = lax.dot_general(q.astype(BF16), k_ref[...], (((1,), (1,)), ((), ())),
                        preferred_element_type=F32)
    rs = _row_scale(ssq_ref[...], hq_ref.shape[1]) * scale
    s = s * jnp.concatenate([rs] * (s.shape[1] // LANES), axis=1)
    s = s - jnp.max(s, axis=-1, keepdims=True)
    e = jnp.exp(s)
    p = e / jnp.sum(e, axis=-1, keepdims=True)
    o_ref[...] = jnp.dot(p.astype(BF16), v_ref[...],
                         preferred_element_type=F32).astype(o_ref.dtype)


def _xattn(hq, ssq, wq, k, v, *, seq_len, n_mem):
    m, d = hq.shape
    hd = d // XATTN_HEADS
    tm = XATTN_ROWS
    assert seq_len % tm == 0 and n_mem % LANES == 0
    tiles_per_seq = seq_len // tm
    kern = functools.partial(_xattn_kernel, scale=float(hd) ** -0.5)
    return pl.pallas_call(
        kern,
        grid=(m // tm, XATTN_HEADS),
        in_specs=[pl.BlockSpec((tm, d), lambda i, h: (i, 0)),
                  pl.BlockSpec((tm, LANES), lambda i, h: (i, 0)),
                  pl.BlockSpec((d, hd), lambda i, h: (0, h)),
                  pl.BlockSpec((n_mem, hd), lambda i, h: (i // tiles_per_seq, h)),
                  pl.BlockSpec((n_mem, hd), lambda i, h: (i // tiles_per_seq, h))],
        out_specs=pl.BlockSpec((tm, hd), lambda i, h: (i, h)),
        out_shape=jax.ShapeDtypeStruct((m, d), BF16),
        compiler_params=_cparams("arbitrary", "arbitrary"),
        name="xattn",
    )(hq, ssq, wq, k, v)


def _ffn_up_kernel(h_ref, ssq_ref, wa_ref, wg_ref, cw_ref, cb_ref, o_ref,
                   wab_ref, wgb_ref, a0_ref, g0_ref, a1_ref, g1_ref, stage_ref, halo_ref,
                   *, tm, n_i, n_steps, tiles_per_seq, n_phase):
    t = pl.program_id(0)
    i = t % n_i

    @pl.when(jnp.logical_and(i == 0, t < n_steps - 1))
    def _():
        wab_ref[...] = wa_ref[...].astype(BF16)
        wgb_ref[...] = wg_ref[...].astype(BF16)

    mc = tm // n_phase
    rc = BF16_ROWS
    n_cols = o_ref.shape[1] // LANES
    tail_seq_start = (jnp.maximum(t - 1, 0) % n_i) % tiles_per_seq == 0

    def tail_piece(a_prev, g_prev, r0, col, above, zero):
        lanes = pl.ds(col * LANES, LANES)
        rows = pl.ds(r0, rc)
        rs = _row_scale(ssq_ref[rows, :], h_ref.shape[1])
        a = _after(a_prev[rows, lanes], zero) * rs
        ext = jnp.concatenate([above, a], axis=0)
        conv = (cw_ref[0:1, lanes] * ext[SUBLANES - 2:SUBLANES - 2 + rc]
                + cw_ref[1:2, lanes] * ext[SUBLANES - 1:SUBLANES - 1 + rc]
                + cw_ref[2:3, lanes] * a) + cb_ref[:, lanes]
        out = jax.nn.silu(conv) * (g_prev[rows, lanes] * rs)
        stage_ref[rows, lanes] = out.astype(stage_ref.dtype)
        return a[rc - SUBLANES:], _chain_zero(out)

    def step(cur, prev, dots, tail):
        kc = h_ref.shape[1] // n_phase
        a_acc = g_acc = None
        zeros = [None] * n_cols
        if tail:
            above = [jnp.where(tail_seq_start, 0.0, halo_ref[col]) for col in range(n_cols)]
        for p in range(n_phase):
            if dots:
                hb = h_ref[:, pl.ds(p * kc, kc)]
                pa = jnp.dot(hb, wab_ref[pl.ds(p * kc, kc), :], preferred_element_type=F32)
                pg = jnp.dot(hb, wgb_ref[pl.ds(p * kc, kc), :], preferred_element_type=F32)
                a_acc = pa if a_acc is None else a_acc + pa
                g_acc = pg if g_acc is None else g_acc + pg
            if tail:
                for r0 in range(p * mc, (p + 1) * mc, rc):
                    for col in range(n_cols):
                        above[col], zeros[col] = tail_piece(prev[0], prev[1], r0, col,
                                                            above[col], zeros[col])
        if tail:
            for col in range(n_cols):
                halo_ref[col] = above[col]
            o_ref[...] = stage_ref[...]
        if dots:
            cur[0][...] = a_acc
            cur[1][...] = g_acc

    slots = ((a0_ref, g0_ref), (a1_ref, g1_ref))
    middle = jnp.logical_and(t > 0, t < n_steps - 1)

    @pl.when(t == 0)
    def _():
        halo_ref[...] = jnp.zeros_like(halo_ref)
        step(slots[0], None, dots=True, tail=False)

    for parity in (0, 1):
        @pl.when(jnp.logical_and(middle, t % 2 == parity))
        def _():
            step(slots[parity], slots[1 - parity], dots=True, tail=True)

    @pl.when(t == n_steps - 1)
    def _():
        step(None, slots[(n_steps - 2) % 2], dots=False, tail=True)


def _ffn_up(h, ssq, w_up, conv_w, conv_b, *, seq_len):
    m, d = h.shape
    d_ff = conv_w.shape[-1]
    tm, tf = FFN_UP_TILE
    assert seq_len % tm == 0 and d_ff % tf == 0 and w_up.shape == (d, 2 * d_ff)
    assert d % OVERLAP_PHASES == 0 and tm % (OVERLAP_PHASES * BF16_ROWS) == 0
    nf = d_ff // tf
    n_i = m // tm
    n_steps = nf * n_i + 1
    kern = functools.partial(_ffn_up_kernel, tm=tm, n_i=n_i, n_steps=n_steps,
                             n_phase=OVERLAP_PHASES, tiles_per_seq=seq_len // tm)

    def tile_j(t):
        return jnp.minimum(t, n_steps - 2) // n_i

    def tail_i(t):
        return jnp.maximum(t - 1, 0) % n_i

    def tail_j(t):
        return jnp.maximum(t - 1, 0) // n_i

    return pl.pallas_call(
        kern,
        grid=(n_steps,),
        in_specs=[pl.BlockSpec((tm, d), lambda t: (jnp.minimum(t, n_steps - 2) % n_i, 0)),
                  pl.BlockSpec((tm, LANES), lambda t: (tail_i(t), 0)),
                  pl.BlockSpec((d, tf), lambda t: (0, tile_j(t))),
                  pl.BlockSpec((d, tf), lambda t: (0, tile_j(t) + nf)),
                  pl.BlockSpec((3, tf), lambda t: (0, tail_j(t))),
                  pl.BlockSpec((1, tf), lambda t: (0, tail_j(t)))],
        out_specs=pl.BlockSpec((tm, tf), lambda t: (tail_i(t), tail_j(t))),
        out_shape=jax.ShapeDtypeStruct((m, d_ff), BF16),
        scratch_shapes=[pltpu.VMEM((d, tf), BF16), pltpu.VMEM((d, tf), BF16),
                        pltpu.VMEM((tm, tf), F32), pltpu.VMEM((tm, tf), F32),
                        pltpu.VMEM((tm, tf), F32), pltpu.VMEM((tm, tf), F32),
                        pltpu.VMEM((tm, tf), BF16),
                        pltpu.VMEM((tf // LANES, SUBLANES, LANES), F32)],
        compiler_params=_cparams("arbitrary"),
        name="ffn_up",
    )(h, ssq, w_up, w_up, conv_w.astype(F32), conv_b.reshape(1, d_ff).astype(F32))


def kernel(x, mem, norm_mix_g, w_in, ssm_lambda_re, ssm_lambda_im, ssm_log_step, ssm_b_re, ssm_b_im, ssm_c_re, ssm_c_im, ssm_d, ssm_glu_w, ssm_glu_b, conv_w, out_norm_ssm_g, out_norm_conv_g, w_out, norm_xattn_g, norm_mem_g, xattn_wq, xattn_wk, xattn_wv, xattn_wo, norm_ffn_g, ffn_w_up, ffn_conv_w, ffn_conv_b, ffn_w_down, norm_final_g):
    bsz, seq_len, d_model = x.shape
    n_mem = mem.shape[1]
    depth = w_in.shape[0]
    m = bsz * seq_len
    xs = x.reshape(m, d_model)
    mems = mem.reshape(bsz * n_mem, d_model)
    for l in range(depth):
        h = _rmsnorm(xs, norm_mix_g[l], BF16)
        proj = _matmul(h, w_in[l].astype(BF16), tile=PROJ_IN_TILE, out_dtype=BF16, name="proj_in")
        wb, wc, cp, d = _ssm_params(ssm_lambda_re[l], ssm_lambda_im[l], ssm_log_step[l],
                                    ssm_b_re[l], ssm_b_im[l], ssm_c_re[l], ssm_c_im[l], ssm_d[l])
        z = _ssm(proj, wb, wc, cp, d, seq_len=seq_len)
        mixed = _mix(z, proj, ssm_glu_w[l].astype(BF16), ssm_glu_b[l], out_norm_ssm_g[l],
                     conv_w[l], out_norm_conv_g[l], seq_len=seq_len)
        xs, hq, ssq = _matmul_res_norm(mixed, w_out[l].astype(BF16), xs, norm_xattn_g[l],
                                       name="proj_out")
        hm = _rmsnorm(mems, norm_mem_g[l], BF16)
        k = _matmul(hm, xattn_wk[l], tile=KV_TILE, out_dtype=BF16, name="proj_k")
        v = _matmul(hm, xattn_wv[l], tile=KV_TILE, out_dtype=BF16, name="proj_v")
        o = _xattn(hq, ssq, xattn_wq[l].astype(BF16), k, v, seq_len=seq_len, n_mem=n_mem)
        xs, h, ssq = _matmul_res_norm(o, xattn_wo[l].astype(BF16), xs, norm_ffn_g[l],
                                      name="proj_o")
        act = _ffn_up(h, ssq, ffn_w_up[l], ffn_conv_w[l], ffn_conv_b[l], seq_len=seq_len)
        xs = _matmul(act, ffn_w_down[l].astype(BF16), tile=FFN_DOWN_TILE, out_dtype=F32,
                     residual=xs, name="ffn_down")
    out = _rmsnorm(xs, norm_final_g, x.dtype)
    return out.reshape(bsz, seq_len, d_model)
```

```python
import functools

import jax
import jax.numpy as jnp
from jax import lax
from jax.experimental import pallas as pl
from jax.experimental.pallas import tpu as pltpu

F32 = jnp.float32
BF16 = jnp.bfloat16

EPS = 1e-6
SSM_GROUP = 16
SSM_STATE = 64
XATTN_HEADS = 4

LANES = 128
SUBLANES = 8
BF16_ROWS = 2 * SUBLANES
V7X_VMEM_LIMIT_BYTES = 56 * 1024 * 1024

GROUPS_PER_BLOCK = LANES // SSM_GROUP
STATE_LANES = GROUPS_PER_BLOCK * SSM_STATE
SCAN_SUB = 32
SCAN_UNIT = SUBLANES * SCAN_SUB

NORM_ROWS = 512
PROJ_IN_TILE = (1024, 1024)
PROJ_RES_TILE = (1024, 512)
KV_TILE = (1024, 512)
XATTN_ROWS = 1024
MIX_ROWS = 256
SSM_ROWS = 4096
FFN_UP_TILE = (1024, 256)
FFN_DOWN_TILE = (512, 512)
OVERLAP_PHASES = 4
MIX_CHAINS = 2


def _cparams(*sem):
    return pltpu.CompilerParams(dimension_semantics=sem,
                                vmem_limit_bytes=V7X_VMEM_LIMIT_BYTES)


def _rms_scale(xf):
    return lax.rsqrt(jnp.mean(xf * xf, axis=-1, keepdims=True) + EPS)


def _row_scale(ssq, width):
    return lax.rsqrt(ssq * (1.0 / width) + EPS)


def _chain_zero(x):
    return (lax.bitcast_convert_type(x, jnp.uint32) >> 16) >> 16


def _after(x, zero):
    if zero is None:
        return x
    return lax.bitcast_convert_type(lax.bitcast_convert_type(x, jnp.uint32) + zero, x.dtype)


def _rmsnorm_kernel(x_ref, g_ref, o_ref):
    xf = x_ref[...].astype(F32)
    o_ref[...] = (xf * _rms_scale(xf) * g_ref[...]).astype(o_ref.dtype)


def _rmsnorm(x, g, out_dtype):
    m, d = x.shape
    tm = NORM_ROWS
    assert m % tm == 0
    return pl.pallas_call(
        _rmsnorm_kernel,
        grid=(m // tm,),
        in_specs=[pl.BlockSpec((tm, d), lambda i: (i, 0)),
                  pl.BlockSpec((1, d), lambda i: (0, 0))],
        out_specs=pl.BlockSpec((tm, d), lambda i: (i, 0)),
        out_shape=jax.ShapeDtypeStruct((m, d), out_dtype),
        compiler_params=_cparams("arbitrary"),
        name="rmsnorm",
    )(x, g.reshape(1, d).astype(F32))


def _mm_kernel(a_ref, b_ref, o_ref):
    o_ref[...] = jnp.dot(a_ref[...], b_ref[...],
                         preferred_element_type=F32).astype(o_ref.dtype)


def _mm_res_kernel(a_ref, b_ref, r_ref, o_ref):
    acc = jnp.dot(a_ref[...], b_ref[...], preferred_element_type=F32)
    o_ref[...] = (acc + r_ref[...]).astype(o_ref.dtype)


def _mm_castb_kernel(a_ref, b_ref, o_ref):
    o_ref[...] = jnp.dot(a_ref[...], b_ref[...].astype(BF16),
                         preferred_element_type=F32).astype(o_ref.dtype)


def _matmul(a, b, *, tile, out_dtype, residual=None, name):
    m, k = a.shape
    _, n = b.shape
    tm, tn = min(tile[0], m), tile[1]
    assert m % tm == 0 and n % tn == 0
    in_specs = [pl.BlockSpec((tm, k), lambda i, j: (i, 0)),
                pl.BlockSpec((k, tn), lambda i, j: (0, j))]
    args = [a, b]
    kern = _mm_kernel if b.dtype == BF16 else _mm_castb_kernel
    if residual is not None:
        assert b.dtype == BF16
        in_specs.append(pl.BlockSpec((tm, tn), lambda i, j: (i, j)))
        args.append(residual)
        kern = _mm_res_kernel
    return pl.pallas_call(
        kern,
        grid=(m // tm, n // tn),
        in_specs=in_specs,
        out_specs=pl.BlockSpec((tm, tn), lambda i, j: (i, j)),
        out_shape=jax.ShapeDtypeStruct((m, n), out_dtype),
        compiler_params=_cparams("arbitrary", "arbitrary"),
        name=name,
    )(*args)


def _mm_res_norm_kernel(a_ref, b_ref, r_ref, g_ref, o_ref, xg_ref, ssq_ref):
    x = jnp.dot(a_ref[...], b_ref[...], preferred_element_type=F32) + r_ref[...]
    o_ref[...] = x
    xg_ref[...] = (x * g_ref[...]).astype(xg_ref.dtype)
    part = jnp.broadcast_to(jnp.sum(x * x, axis=-1, keepdims=True), ssq_ref.shape)

    @pl.when(pl.program_id(1) == 0)
    def _():
        ssq_ref[...] = part

    @pl.when(pl.program_id(1) > 0)
    def _():
        ssq_ref[...] += part


def _matmul_res_norm(a, b, residual, gain, *, name):
    m, k = a.shape
    _, n = b.shape
    tm, tn = PROJ_RES_TILE
    assert m % tm == 0 and n % tn == 0
    return pl.pallas_call(
        _mm_res_norm_kernel,
        grid=(m // tm, n // tn),
        in_specs=[pl.BlockSpec((tm, k), lambda i, j: (i, 0)),
                  pl.BlockSpec((k, tn), lambda i, j: (0, j)),
                  pl.BlockSpec((tm, tn), lambda i, j: (i, j)),
                  pl.BlockSpec((1, tn), lambda i, j: (0, j))],
        out_specs=[pl.BlockSpec((tm, tn), lambda i, j: (i, j)),
                   pl.BlockSpec((tm, tn), lambda i, j: (i, j)),
                   pl.BlockSpec((tm, LANES), lambda i, j: (i, 0))],
        out_shape=[jax.ShapeDtypeStruct((m, n), F32),
                   jax.ShapeDtypeStruct((m, n), BF16),
                   jax.ShapeDtypeStruct((m, LANES), F32)],
        compiler_params=_cparams("arbitrary", "arbitrary"),
        name=name,
    )(a, b, residual, gain.reshape(1, n).astype(F32))


def _cmul(ar, ai, br, bi):
    return ar * br - ai * bi, ar * bi + ai * br


def _shift_rows_down(x, k):
    rows = lax.broadcasted_iota(jnp.int32, x.shape, 0)
    return jnp.where(rows >= k, pltpu.roll(x, k, 0), 0.0)


def _ssm_kernel(u_ref, perm_ref, unperm_ref, wb_ref, wc_ref, cp_ref, cps_ref, cpb_ref, d_ref,
                z_ref, s_all_ref, carry_ref, *, units_per_step, steps_per_seq):
    sl = STATE_LANES
    re = pl.ds(0, sl)
    im = pl.ds(sl, sl)

    @pl.when(pl.program_id(1) % steps_per_seq == 0)
    def _():
        carry_ref[...] = jnp.zeros_like(carry_ref)

    def rows_of(i):
        return pl.ds(i * SUBLANES, SUBLANES)

    def const(k):
        rows = pl.ds(k * SUBLANES, SUBLANES)
        return cp_ref[0, rows, re], cp_ref[0, rows, im]

    base = SCAN_SUB

    def project_in(un):
        u = jnp.dot(perm_ref[...], u_ref[pl.ds(un * SCAN_UNIT, SCAN_UNIT), :],
                    preferred_element_type=F32).astype(BF16)
        s_all_ref[un] = jnp.dot(u, wb_ref[0], preferred_element_type=F32)
        return u

    a_re, a_im = const(0)
    c_re = carry_ref[:, re]
    c_im = carry_ref[:, im]
    u_next = project_in(0)
    for un in range(units_per_step):
        s_ref = s_all_ref.at[un]
        u = u_next
        if un + 1 < units_per_step:
            u_next = project_in(un + 1)

        h_re = jnp.zeros((SUBLANES, sl), F32)
        h_im = jnp.zeros((SUBLANES, sl), F32)
        for i in range(SCAN_SUB):
            p_re, p_im = _cmul(a_re, a_im, h_re, h_im)
            h_re = p_re + s_ref[rows_of(i), re]
            h_im = p_im + s_ref[rows_of(i), im]
            s_ref[rows_of(i), re] = h_re
            s_ref[rows_of(i), im] = h_im

        f_re, f_im = h_re, h_im
        for lvl, k in enumerate((1, 2, 4)):
            q_re, q_im = _cmul(*const(base + lvl),
                               _shift_rows_down(f_re, k), _shift_rows_down(f_im, k))
            f_re = f_re + q_re
            f_im = f_im + q_im
        in_re, in_im = _cmul(cps_ref[0, :, re], cps_ref[0, :, im], c_re, c_im)
        in_re = in_re + _shift_rows_down(f_re, 1)
        in_im = in_im + _shift_rows_down(f_im, 1)
        n_re, n_im = _cmul(*const(base + 3), c_re, c_im)
        last = SUBLANES - 1
        c_re = n_re + jnp.broadcast_to(f_re[last:last + 1], (SUBLANES, sl))
        c_im = n_im + jnp.broadcast_to(f_im[last:last + 1], (SUBLANES, sl))

        cin_re = jnp.concatenate([in_re, in_re], axis=0).astype(BF16)
        cin_im = jnp.concatenate([in_im, in_im], axis=0).astype(BF16)
        parts = []
        for i in range(0, SCAN_SUB, 2):
            rows = pl.ds(i * SUBLANES, BF16_ROWS)
            p_re, p_im = _cmul(cpb_ref[0, rows, re], cpb_ref[0, rows, im], cin_re, cin_im)
            parts.append(s_ref[rows, :].astype(BF16) + jnp.concatenate([p_re, p_im], axis=1))
        y = jnp.dot(jnp.concatenate(parts, axis=0), wc_ref[0], preferred_element_type=F32)
        y = y + d_ref[0] * u.astype(F32)
        z = jax.nn.gelu(y).astype(BF16)
        z_ref[pl.ds(un * SCAN_UNIT, SCAN_UNIT), :] = jnp.dot(
            unperm_ref[...], z, preferred_element_type=F32).astype(z_ref.dtype)
    carry_ref[:, re] = c_re
    carry_ref[:, im] = c_im


def _ssm(proj, wb, wc, cp, cps, d, *, seq_len):
    m = proj.shape[0]
    nb = wb.shape[0]
    tt = min(SSM_ROWS, seq_len)
    assert seq_len % tt == 0 and tt % SCAN_UNIT == 0
    kern = functools.partial(_ssm_kernel, units_per_step=tt // SCAN_UNIT,
                             steps_per_seq=seq_len // tt)
    r = jnp.arange(SCAN_UNIT)
    src = (r % SUBLANES) * SCAN_SUB + r // SUBLANES
    perm = (src[:, None] == r[None, :]).astype(BF16)
    return pl.pallas_call(
        kern,
        grid=(nb, m // tt),
        in_specs=[pl.BlockSpec((tt, LANES), lambda b, t: (t, b)),
                  pl.BlockSpec((SCAN_UNIT, SCAN_UNIT), lambda b, t: (0, 0)),
                  pl.BlockSpec((SCAN_UNIT, SCAN_UNIT), lambda b, t: (0, 0)),
                  pl.BlockSpec((1, LANES, 2 * STATE_LANES), lambda b, t: (b, 0, 0)),
                  pl.BlockSpec((1, 2 * STATE_LANES, LANES), lambda b, t: (b, 0, 0)),
                  pl.BlockSpec((1,) + cp.shape[1:], lambda b, t: (b, 0, 0)),
                  pl.BlockSpec((1, SUBLANES, 2 * STATE_LANES), lambda b, t: (b, 0, 0)),
                  pl.BlockSpec((1, SCAN_UNIT, 2 * STATE_LANES), lambda b, t: (b, 0, 0)),
                  pl.BlockSpec((1, 1, LANES), lambda b, t: (b, 0, 0))],
        out_specs=pl.BlockSpec((tt, LANES), lambda b, t: (t, b)),
        out_shape=jax.ShapeDtypeStruct((m, nb * LANES), BF16),
        scratch_shapes=[pltpu.VMEM((tt // SCAN_UNIT, SCAN_UNIT, 2 * STATE_LANES), F32),
                        pltpu.VMEM((SUBLANES, 2 * STATE_LANES), F32)],
        compiler_params=_cparams("arbitrary", "arbitrary"),
        name="s5_scan",
    )(proj, perm, perm.T, wb, wc, cp, cps, cp[:, :SCAN_UNIT].astype(BF16), d)


def _ssm_params(lam_re, lam_im, log_step, b_re, b_im, c_re, c_im, d):
    g, p = lam_re.shape
    h = b_re.shape[-1]
    assert (h, p) == (SSM_GROUP, SSM_STATE) and g % GROUPS_PER_BLOCK == 0
    nb = g // GROUPS_PER_BLOCK
    lr = lam_re.astype(F32)
    li = lam_im.astype(F32)
    step = jnp.exp(log_step.astype(F32))[:, None]
    mag = jnp.exp(lr * step)
    ang = li * step
    abar_re, abar_im = mag * jnp.cos(ang), mag * jnp.sin(ang)
    den = lr * lr + li * li
    nr, ni = abar_re - 1.0, abar_im
    coef_re = (nr * lr + ni * li) / den
    coef_im = (ni * lr - nr * li) / den
    br_f, bi_f = b_re.astype(F32), b_im.astype(F32)
    bbar_re = coef_re[..., None] * br_f - coef_im[..., None] * bi_f
    bbar_im = coef_re[..., None] * bi_f + coef_im[..., None] * br_f
    eye = jnp.eye(GROUPS_PER_BLOCK, dtype=F32)

    def in_block(x):
        x = x.reshape(nb, GROUPS_PER_BLOCK, p, h).transpose(0, 1, 3, 2)
        x = jnp.einsum('lghp,gk->lghkp', x, eye)
        return x.reshape(nb, LANES, STATE_LANES)

    def out_block(x):
        x = x.reshape(nb, GROUPS_PER_BLOCK, h, p).transpose(0, 1, 3, 2)
        x = jnp.einsum('lgph,gk->lgpkh', x, eye)
        return x.reshape(nb, STATE_LANES, LANES)

    wb = jnp.concatenate([in_block(bbar_re), in_block(bbar_im)], axis=-1).astype(BF16)
    wc = jnp.concatenate([out_block(c_re.astype(F32)), out_block(-c_im.astype(F32))],
                         axis=1).astype(BF16)

    a_re = abar_re.reshape(nb, STATE_LANES)
    a_im = abar_im.reshape(nb, STATE_LANES)
    pows = [(a_re, a_im)]
    for _ in range(SCAN_SUB - 1):
        pows.append(_cmul(a_re, a_im, *pows[-1]))
    sub = pows[-1]
    one = (jnp.ones_like(a_re), jnp.zeros_like(a_im))
    per_sublane = [one]
    for _ in range(SUBLANES - 1):
        per_sublane.append(_cmul(*sub, *per_sublane[-1]))
    sub2 = _cmul(*sub, *sub)
    sub4 = _cmul(*sub2, *sub2)
    sub8 = _cmul(*sub4, *sub4)

    def table(pairs):
        return jnp.stack([jnp.concatenate(c, axis=-1) for c in pairs], axis=1)

    cp = jnp.repeat(table(pows + [sub, sub2, sub4, sub8]), SUBLANES, axis=1)
    return wb, wc, cp, table(per_sublane), d.astype(F32).reshape(nb, 1, LANES)


def _mix_kernel(z_ref, gw_ref, gb_ref, ng_s_ref, pb_ref, pc_ref, pv_ref, hc_ref, hv_ref,
                cw_ref, ng_c_ref, o_ref, lin_ref, yc_ref, sq_ref,
                *, tm, width, tiles_per_seq, n_chain):
    z = z_ref[...]
    lin_ref[...] = jnp.dot(z, gw_ref[...], preferred_element_type=F32)

    n_cols = width // LANES
    cols_per_chain = n_cols // n_chain
    rc = BF16_ROWS
    first = pl.program_id(0) % tiles_per_seq == 0
    above = []
    for col in range(n_cols):
        lanes = pl.ds(col * LANES, LANES)
        halo = hc_ref[:, lanes].astype(F32) * hv_ref[:, lanes].astype(F32)
        above.append(jnp.where(first, 0.0, halo))
    zeros = [None] * n_chain
    for r0 in range(0, tm, rc):
        rows = pl.ds(r0, rc)
        sq = [None] * n_chain
        for k in range(cols_per_chain):
            for c in range(n_chain):
                col = c * cols_per_chain + k
                lanes = pl.ds(col * LANES, LANES)
                cv = _after(pc_ref[rows, lanes].astype(F32), zeros[c]) \
                    * pv_ref[rows, lanes].astype(F32)
                ext = jnp.concatenate([above[col], cv], axis=0)
                conv = (cw_ref[0:1, lanes] * ext[SUBLANES - 2:SUBLANES - 2 + rc]
                        + cw_ref[1:2, lanes] * ext[SUBLANES - 1:SUBLANES - 1 + rc]
                        + cw_ref[2:3, lanes] * cv)
                yc = pb_ref[rows, lanes].astype(F32) * conv
                yc_ref[rows, lanes] = yc
                sq[c] = yc * yc if sq[c] is None else sq[c] + yc * yc
                above[col] = cv[rc - SUBLANES:]
                zeros[c] = _chain_zero(yc)
        total = sq[0]
        for part in sq[1:]:
            total = total + part
        sq_ref[rows, :] = total

    y = z.astype(F32) * jax.nn.sigmoid(lin_ref[...] + gb_ref[...])
    o_ref[:, pl.ds(0, width)] = (y * _rms_scale(y) * ng_s_ref[...]).astype(o_ref.dtype)

    ms = jnp.sum(sq_ref[...], axis=-1, keepdims=True) * (1.0 / width)
    o_ref[:, pl.ds(width, width)] = (yc_ref[...] * lax.rsqrt(ms + EPS)
                                     * ng_c_ref[...]).astype(o_ref.dtype)


def _mix(z, proj, glu_w, glu_b, ng_s, conv_w, ng_c, *, seq_len):
    m, width = z.shape
    tm = MIX_ROWS
    assert proj.shape[1] == 4 * width and seq_len % tm == 0
    assert (width // LANES) % MIX_CHAINS == 0
    halo_blocks = tm // SUBLANES
    kern = functools.partial(_mix_kernel, tm=tm, width=width, tiles_per_seq=seq_len // tm,
                             n_chain=MIX_CHAINS)
    row = lambda a: a.reshape(1, width).astype(F32)

    def halo_map(col):
        return lambda i: (jnp.maximum(i * halo_blocks - 1, 0), col)

    return pl.pallas_call(
        kern,
        grid=(m // tm,),
        in_specs=[pl.BlockSpec((tm, width), lambda i: (i, 0)),
                  pl.BlockSpec((width, width), lambda i: (0, 0)),
                  pl.BlockSpec((1, width), lambda i: (0, 0)),
                  pl.BlockSpec((1, width), lambda i: (0, 0)),
                  pl.BlockSpec((tm, width), lambda i: (i, 1)),
                  pl.BlockSpec((tm, width), lambda i: (i, 2)),
                  pl.BlockSpec((tm, width), lambda i: (i, 3)),
                  pl.BlockSpec((SUBLANES, width), halo_map(2)),
                  pl.BlockSpec((SUBLANES, width), halo_map(3)),
                  pl.BlockSpec((3, width), lambda i: (0, 0)),
                  pl.BlockSpec((1, width), lambda i: (0, 0))],
        out_specs=pl.BlockSpec((tm, 2 * width), lambda i: (i, 0)),
        out_shape=jax.ShapeDtypeStruct((m, 2 * width), BF16),
        scratch_shapes=[pltpu.VMEM((tm, width), F32), pltpu.VMEM((tm, width), F32),
                        pltpu.VMEM((tm, LANES), F32)],
        compiler_params=_cparams("arbitrary"),
        name="mixer_tail",
    )(z, glu_w, row(glu_b), row(ng_s), proj, proj, proj, proj, proj,
      conv_w.astype(F32), row(ng_c))


def _xattn_kernel(hq_ref, ssq_ref, wq_ref, k_ref, v_ref, o_ref, *, scale):
    q = jnp.dot(hq_ref[...], wq_ref[...], preferred_element_type=F32)
    s = lax.dot_general(q.astype(BF16), k_ref[...], (((1,), (1,)), ((), ())),
                        preferred_element_type=F32)
    rs = _row_scale(ssq_ref[...], hq_ref.shape[1]) * scale
    s = s * jnp.concatenate([rs] * (s.shape[1] // LANES), axis=1)
    s = s - jnp.max(s, axis=-1, keepdims=True)
    e = jnp.exp(s)
    p = e / jnp.sum(e, axis=-1, keepdims=True)
    o_ref[...] = jnp.dot(p.astype(BF16), v_ref[...],
                         preferred_element_type=F32).astype(o_ref.dtype)


def _xattn(hq, ssq, wq, k, v, *, seq_len, n_mem):
    m, d = hq.shape
    hd = d // XATTN_HEADS
    tm = XATTN_ROWS
    assert seq_len % tm == 0 and n_mem % LANES == 0
    tiles_per_seq = seq_len // tm
    kern = functools.partial(_xattn_kernel, scale=float(hd) ** -0.5)
    return pl.pallas_call(
        kern,
        grid=(m // tm, XATTN_HEADS),
        in_specs=[pl.BlockSpec((tm, d), lambda i, h: (i, 0)),
                  pl.BlockSpec((tm, LANES), lambda i, h: (i, 0)),
                  pl.BlockSpec((d, hd), lambda i, h: (0, h)),
                  pl.BlockSpec((n_mem, hd), lambda i, h: (i // tiles_per_seq, h)),
                  pl.BlockSpec((n_mem, hd), lambda i, h: (i // tiles_per_seq, h))],
        out_specs=pl.BlockSpec((tm, hd), lambda i, h: (i, h)),
        out_shape=jax.ShapeDtypeStruct((m, d), BF16),
        compiler_params=_cparams("arbitrary", "arbitrary"),
        name="xattn",
    )(hq, ssq, wq, k, v)


def _ffn_up_kernel(h_ref, ssq_ref, wa_ref, wg_ref, cw_ref, cb_ref, o_ref,
                   wab_ref, wgb_ref, a0_ref, g0_ref, a1_ref, g1_ref, stage_ref, halo_ref,
                   *, tm, n_i, n_steps, tiles_per_seq, n_phase):
    t = pl.program_id(0)
    i = t % n_i

    @pl.when(jnp.logical_and(i == 0, t < n_steps - 1))
    def _():
        wab_ref[...] = wa_ref[...].astype(BF16)
        wgb_ref[...] = wg_ref[...].astype(BF16)

    mc = tm // n_phase
    rc = BF16_ROWS
    n_cols = o_ref.shape[1] // LANES
    tail_seq_start = (jnp.maximum(t - 1, 0) % n_i) % tiles_per_seq == 0

    def tail_piece(a_prev, g_prev, r0, col, above, zero):
        lanes = pl.ds(col * LANES, LANES)
        rows = pl.ds(r0, rc)
        rs = _row_scale(ssq_ref[rows, :], h_ref.shape[1])
        a = _after(a_prev[rows, lanes], zero) * rs
        ext = jnp.concatenate([above, a], axis=0)
        conv = (cw_ref[0:1, lanes] * ext[SUBLANES - 2:SUBLANES - 2 + rc]
                + cw_ref[1:2, lanes] * ext[SUBLANES - 1:SUBLANES - 1 + rc]
                + cw_ref[2:3, lanes] * a) + cb_ref[:, lanes]
        out = jax.nn.silu(conv) * (g_prev[rows, lanes] * rs)
        stage_ref[rows, lanes] = out.astype(stage_ref.dtype)
        return a[rc - SUBLANES:], _chain_zero(out)

    def step(cur, prev, dots, tail):
        kc = h_ref.shape[1] // n_phase
        a_acc = g_acc = None
        zeros = [None] * n_cols
        if tail:
            above = [jnp.where(tail_seq_start, 0.0, halo_ref[col]) for col in range(n_cols)]
        for p in range(n_phase):
            if dots:
                hb = h_ref[:, pl.ds(p * kc, kc)]
                pa = jnp.dot(hb, wab_ref[pl.ds(p * kc, kc), :], preferred_element_type=F32)
                pg = jnp.dot(hb, wgb_ref[pl.ds(p * kc, kc), :], preferred_element_type=F32)
                a_acc = pa if a_acc is None else a_acc + pa
                g_acc = pg if g_acc is None else g_acc + pg
            if tail:
                for r0 in range(p * mc, (p + 1) * mc, rc):
                    for col in range(n_cols):
                        above[col], zeros[col] = tail_piece(prev[0], prev[1], r0, col,
                                                            above[col], zeros[col])
        if tail:
            for col in range(n_cols):
                halo_ref[col] = above[col]
            o_ref[...] = stage_ref[...]
        if dots:
            cur[0][...] = a_acc
            cur[1][...] = g_acc

    slots = ((a0_ref, g0_ref), (a1_ref, g1_ref))
    middle = jnp.logical_and(t > 0, t < n_steps - 1)

    @pl.when(t == 0)
    def _():
        halo_ref[...] = jnp.zeros_like(halo_ref)
        step(slots[0], None, dots=True, tail=False)

    for parity in (0, 1):
        @pl.when(jnp.logical_and(middle, t % 2 == parity))
        def _():
            step(slots[parity], slots[1 - parity], dots=True, tail=True)

    @pl.when(t == n_steps - 1)
    def _():
        step(None, slots[(n_steps - 2) % 2], dots=False, tail=True)


def _ffn_up(h, ssq, w_up, conv_w, conv_b, *, seq_len):
    m, d = h.shape
    d_ff = conv_w.shape[-1]
    tm, tf = FFN_UP_TILE
    assert seq_len % tm == 0 and d_ff % tf == 0 and w_up.shape == (d, 2 * d_ff)
    assert d % OVERLAP_PHASES == 0 and tm % (OVERLAP_PHASES * BF16_ROWS) == 0
    nf = d_ff // tf
    n_i = m // tm
    n_steps = nf * n_i + 1
    kern = functools.partial(_ffn_up_kernel, tm=tm, n_i=n_i, n_steps=n_steps,
                             n_phase=OVERLAP_PHASES, tiles_per_seq=seq_len // tm)

    def tile_j(t):
        return jnp.minimum(t, n_steps - 2) // n_i

    def tail_i(t):
        return jnp.maximum(t - 1, 0) % n_i

    def tail_j(t):
        return jnp.maximum(t - 1, 0) // n_i

    return pl.pallas_call(
        kern,
        grid=(n_steps,),
        in_specs=[pl.BlockSpec((tm, d), lambda t: (jnp.minimum(t, n_steps - 2) % n_i, 0)),
                  pl.BlockSpec((tm, LANES), lambda t: (tail_i(t), 0)),
                  pl.BlockSpec((d, tf), lambda t: (0, tile_j(t))),
                  pl.BlockSpec((d, tf), lambda t: (0, tile_j(t) + nf)),
                  pl.BlockSpec((3, tf), lambda t: (0, tail_j(t))),
                  pl.BlockSpec((1, tf), lambda t: (0, tail_j(t)))],
        out_specs=pl.BlockSpec((tm, tf), lambda t: (tail_i(t), tail_j(t))),
        out_shape=jax.ShapeDtypeStruct((m, d_ff), BF16),
        scratch_shapes=[pltpu.VMEM((d, tf), BF16), pltpu.VMEM((d, tf), BF16),
                        pltpu.VMEM((tm, tf), F32), pltpu.VMEM((tm, tf), F32),
                        pltpu.VMEM((tm, tf), F32), pltpu.VMEM((tm, tf), F32),
                        pltpu.VMEM((tm, tf), BF16),
                        pltpu.VMEM((tf // LANES, SUBLANES, LANES), F32)],
        compiler_params=_cparams("arbitrary"),
        name="ffn_up",
    )(h, ssq, w_up, w_up, conv_w.astype(F32), conv_b.reshape(1, d_ff).astype(F32))


def kernel(x, mem, norm_mix_g, w_in, ssm_lambda_re, ssm_lambda_im, ssm_log_step, ssm_b_re, ssm_b_im, ssm_c_re, ssm_c_im, ssm_d, ssm_glu_w, ssm_glu_b, conv_w, out_norm_ssm_g, out_norm_conv_g, w_out, norm_xattn_g, norm_mem_g, xattn_wq, xattn_wk, xattn_wv, xattn_wo, norm_ffn_g, ffn_w_up, ffn_conv_w, ffn_conv_b, ffn_w_down, norm_final_g):
    bsz, seq_len, d_model = x.shape
    n_mem = mem.shape[1]
    depth = w_in.shape[0]
    m = bsz * seq_len
    xs = x.reshape(m, d_model)
    mems = mem.reshape(bsz * n_mem, d_model)
    for l in range(depth):
        h = _rmsnorm(xs, norm_mix_g[l], BF16)
        proj = _matmul(h, w_in[l].astype(BF16), tile=PROJ_IN_TILE, out_dtype=BF16, name="proj_in")
        ssm_tables = _ssm_params(ssm_lambda_re[l], ssm_lambda_im[l], ssm_log_step[l], ssm_b_re[l],
                                 ssm_b_im[l], ssm_c_re[l], ssm_c_im[l], ssm_d[l])
        z = _ssm(proj, *ssm_tables, seq_len=seq_len)
        mixed = _mix(z, proj, ssm_glu_w[l].astype(BF16), ssm_glu_b[l], out_norm_ssm_g[l],
                     conv_w[l], out_norm_conv_g[l], seq_len=seq_len)
        xs, hq, ssq = _matmul_res_norm(mixed, w_out[l].astype(BF16), xs, norm_xattn_g[l],
                                       name="proj_out")
        hm = _rmsnorm(mems, norm_mem_g[l], BF16)
        k = _matmul(hm, xattn_wk[l], tile=KV_TILE, out_dtype=BF16, name="proj_k")
        v = _matmul(hm, xattn_wv[l], tile=KV_TILE, out_dtype=BF16, name="proj_v")
        o = _xattn(hq, ssq, xattn_wq[l].astype(BF16), k, v, seq_len=seq_len, n_mem=n_mem)
        xs, h, ssq = _matmul_res_norm(o, xattn_wo[l].astype(BF16), xs, norm_ffn_g[l],
                                      name="proj_o")
        act = _ffn_up(h, ssq, ffn_w_up[l], ffn_conv_w[l], ffn_conv_b[l], seq_len=seq_len)
        xs = _matmul(act, ffn_w_down[l].astype(BF16), tile=FFN_DOWN_TILE, out_dtype=F32,
                     residual=xs, name="ffn_down")
    out = _rmsnorm(xs, norm_final_g, x.dtype)
    return out.reshape(bsz, seq_len, d_model)
```

```python
import functools

import jax
import jax.numpy as jnp
from jax import lax
from jax.experimental import pallas as pl
from jax.experimental.pallas import tpu as pltpu

F32 = jnp.float32
BF16 = jnp.bfloat16

EPS = 1e-6
SSM_GROUP = 16
SSM_STATE = 64
XATTN_HEADS = 4

LANES = 128
SUBLANES = 8
BF16_ROWS = 2 * SUBLANES
V7X_VMEM_LIMIT_BYTES = 56 * 1024 * 1024

GROUPS_PER_BLOCK = LANES // SSM_GROUP
STATE_LANES = GROUPS_PER_BLOCK * SSM_STATE
SCAN_SUB = 32
SCAN_UNIT = SUBLANES * SCAN_SUB

NORM_ROWS = 512
PROJ_IN_TILE = (1024, 1024)
PROJ_RES_TILE = (1024, 512)
KV_TILE = (1024, 512)
XATTN_ROWS = 1024
MIX_ROWS = 512
SSM_ROWS = 4096
FFN_UP_TILE = (1024, 256)
FFN_DOWN_TILE = (512, 512)
OVERLAP_PHASES = 4
MIX_CHAINS = 2


def _cparams(*sem):
    return pltpu.CompilerParams(dimension_semantics=sem,
                                vmem_limit_bytes=V7X_VMEM_LIMIT_BYTES)


def _rms_scale(xf):
    return lax.rsqrt(jnp.mean(xf * xf, axis=-1, keepdims=True) + EPS)


def _row_scale(ssq, width):
    return lax.rsqrt(ssq * (1.0 / width) + EPS)


def _chain_zero(x):
    return (lax.bitcast_convert_type(x, jnp.uint32) >> 16) >> 16


def _after(x, zero):
    if zero is None:
        return x
    return lax.bitcast_convert_type(lax.bitcast_convert_type(x, jnp.uint32) + zero, x.dtype)


def _rmsnorm_kernel(x_ref, g_ref, o_ref):
    xf = x_ref[...].astype(F32)
    o_ref[...] = (xf * _rms_scale(xf) * g_ref[...]).astype(o_ref.dtype)


def _rmsnorm(x, g, out_dtype):
    m, d = x.shape
    tm = NORM_ROWS
    assert m % tm == 0
    return pl.pallas_call(
        _rmsnorm_kernel,
        grid=(m // tm,),
        in_specs=[pl.BlockSpec((tm, d), lambda i: (i, 0)),
                  pl.BlockSpec((1, d), lambda i: (0, 0))],
        out_specs=pl.BlockSpec((tm, d), lambda i: (i, 0)),
        out_shape=jax.ShapeDtypeStruct((m, d), out_dtype),
        compiler_params=_cparams("arbitrary"),
        name="rmsnorm",
    )(x, g.reshape(1, d).astype(F32))


def _mm_kernel(a_ref, b_ref, o_ref):
    o_ref[...] = jnp.dot(a_ref[...], b_ref[...],
                         preferred_element_type=F32).astype(o_ref.dtype)


def _mm_res_kernel(a_ref, b_ref, r_ref, o_ref):
    acc = jnp.dot(a_ref[...], b_ref[...], preferred_element_type=F32)
    o_ref[...] = (acc + r_ref[...]).astype(o_ref.dtype)


def _mm_castb_kernel(a_ref, b_ref, o_ref):
    o_ref[...] = jnp.dot(a_ref[...], b_ref[...].astype(BF16),
                         preferred_element_type=F32).astype(o_ref.dtype)


def _matmul(a, b, *, tile, out_dtype, residual=None, name):
    m, k = a.shape
    _, n = b.shape
    tm, tn = min(tile[0], m), tile[1]
    assert m % tm == 0 and n % tn == 0
    in_specs = [pl.BlockSpec((tm, k), lambda i, j: (i, 0)),
                pl.BlockSpec((k, tn), lambda i, j: (0, j))]
    args = [a, b]
    kern = _mm_kernel if b.dtype == BF16 else _mm_castb_kernel
    if residual is not None:
        assert b.dtype == BF16
        in_specs.append(pl.BlockSpec((tm, tn), lambda i, j: (i, j)))
        args.append(residual)
        kern = _mm_res_kernel
    return pl.pallas_call(
        kern,
        grid=(m // tm, n // tn),
        in_specs=in_specs,
        out_specs=pl.BlockSpec((tm, tn), lambda i, j: (i, j)),
        out_shape=jax.ShapeDtypeStruct((m, n), out_dtype),
        compiler_params=_cparams("arbitrary", "arbitrary"),
        name=name,
    )(*args)


def _mm_res_norm_kernel(a_ref, b_ref, r_ref, g_ref, o_ref, xg_ref, ssq_ref):
    x = jnp.dot(a_ref[...], b_ref[...], preferred_element_type=F32) + r_ref[...]
    o_ref[...] = x
    xg_ref[...] = (x * g_ref[...]).astype(xg_ref.dtype)
    part = jnp.broadcast_to(jnp.sum(x * x, axis=-1, keepdims=True), ssq_ref.shape)

    @pl.when(pl.program_id(1) == 0)
    def _():
        ssq_ref[...] = part

    @pl.when(pl.program_id(1) > 0)
    def _():
        ssq_ref[...] += part


def _matmul_res_norm(a, b, residual, gain, *, name):
    m, k = a.shape
    _, n = b.shape
    tm, tn = PROJ_RES_TILE
    assert m % tm == 0 and n % tn == 0
    return pl.pallas_call(
        _mm_res_norm_kernel,
        grid=(m // tm, n // tn),
        in_specs=[pl.BlockSpec((tm, k), lambda i, j: (i, 0)),
                  pl.BlockSpec((k, tn), lambda i, j: (0, j)),
                  pl.BlockSpec((tm, tn), lambda i, j: (i, j)),
                  pl.BlockSpec((1, tn), lambda i, j: (0, j))],
        out_specs=[pl.BlockSpec((tm, tn), lambda i, j: (i, j)),
                   pl.BlockSpec((tm, tn), lambda i, j: (i, j)),
                   pl.BlockSpec((tm, LANES), lambda i, j: (i, 0))],
        out_shape=[jax.ShapeDtypeStruct((m, n), F32),
                   jax.ShapeDtypeStruct((m, n), BF16),
                   jax.ShapeDtypeStruct((m, LANES), F32)],
        compiler_params=_cparams("arbitrary", "arbitrary"),
        name=name,
    )(a, b, residual, gain.reshape(1, n).astype(F32))


def _cmul(ar, ai, br, bi):
    return ar * br - ai * bi, ar * bi + ai * br


def _shift_rows_down(x, k):
    rows = lax.broadcasted_iota(jnp.int32, x.shape, 0)
    return jnp.where(rows >= k, pltpu.roll(x, k, 0), 0.0)


def _ssm_kernel(u_ref, perm_ref, unperm_ref, wb_ref, wc_ref, cp_ref, cps_ref, cpb_ref, d_ref,
                z_ref, s_all_ref, carry_ref, *, units_per_step, steps_per_seq):
    sl = STATE_LANES
    re = pl.ds(0, sl)
    im = pl.ds(sl, sl)

    @pl.when(pl.program_id(1) % steps_per_seq == 0)
    def _():
        carry_ref[...] = jnp.zeros_like(carry_ref)

    def rows_of(i):
        return pl.ds(i * SUBLANES, SUBLANES)

    def const(k):
        rows = pl.ds(k * SUBLANES, SUBLANES)
        return cp_ref[0, rows, re], cp_ref[0, rows, im]

    base = SCAN_SUB

    def project_in(un):
        u = jnp.dot(perm_ref[...], u_ref[pl.ds(un * SCAN_UNIT, SCAN_UNIT), :],
                    preferred_element_type=F32).astype(BF16)
        s_all_ref[un] = jnp.dot(u, wb_ref[0], preferred_element_type=F32)
        return u

    a_re, a_im = const(0)
    c_re = carry_ref[:, re]
    c_im = carry_ref[:, im]
    u_next = project_in(0)
    for un in range(units_per_step):
        s_ref = s_all_ref.at[un]
        u = u_next
        if un + 1 < units_per_step:
            u_next = project_in(un + 1)

        h_re = jnp.zeros((SUBLANES, sl), F32)
        h_im = jnp.zeros((SUBLANES, sl), F32)
        for i in range(SCAN_SUB):
            p_re, p_im = _cmul(a_re, a_im, h_re, h_im)
            h_re = p_re + s_ref[rows_of(i), re]
            h_im = p_im + s_ref[rows_of(i), im]
            s_ref[rows_of(i), re] = h_re
            s_ref[rows_of(i), im] = h_im

        f_re, f_im = h_re, h_im
        for lvl, k in enumerate((1, 2, 4)):
            q_re, q_im = _cmul(*const(base + lvl),
                               _shift_rows_down(f_re, k), _shift_rows_down(f_im, k))
            f_re = f_re + q_re
            f_im = f_im + q_im
        in_re, in_im = _cmul(cps_ref[0, :, re], cps_ref[0, :, im], c_re, c_im)
        in_re = in_re + _shift_rows_down(f_re, 1)
        in_im = in_im + _shift_rows_down(f_im, 1)
        n_re, n_im = _cmul(*const(base + 3), c_re, c_im)
        last = SUBLANES - 1
        c_re = n_re + jnp.broadcast_to(f_re[last:last + 1], (SUBLANES, sl))
        c_im = n_im + jnp.broadcast_to(f_im[last:last + 1], (SUBLANES, sl))

        cin_re = jnp.concatenate([in_re, in_re], axis=0).astype(BF16)
        cin_im = jnp.concatenate([in_im, in_im], axis=0).astype(BF16)
        parts = []
        for i in range(0, SCAN_SUB, 2):
            rows = pl.ds(i * SUBLANES, BF16_ROWS)
            p_re, p_im = _cmul(cpb_ref[0, rows, re], cpb_ref[0, rows, im], cin_re, cin_im)
            parts.append(s_ref[rows, :].astype(BF16) + jnp.concatenate([p_re, p_im], axis=1))
        y = jnp.dot(jnp.concatenate(parts, axis=0), wc_ref[0], preferred_element_type=F32)
        y = y + d_ref[0] * u.astype(F32)
        z = jax.nn.gelu(y).astype(BF16)
        z_ref[pl.ds(un * SCAN_UNIT, SCAN_UNIT), :] = jnp.dot(
            unperm_ref[...], z, preferred_element_type=F32).astype(z_ref.dtype)
    carry_ref[:, re] = c_re
    carry_ref[:, im] = c_im


def _ssm(proj, wb, wc, cp, cps, d, *, seq_len):
    m = proj.shape[0]
    nb = wb.shape[0]
    tt = min(SSM_ROWS, seq_len)
    assert seq_len % tt == 0 and tt % SCAN_UNIT == 0
    kern = functools.partial(_ssm_kernel, units_per_step=tt // SCAN_UNIT,
                             steps_per_seq=seq_len // tt)
    r = jnp.arange(SCAN_UNIT)
    src = (r % SUBLANES) * SCAN_SUB + r // SUBLANES
    perm = (src[:, None] == r[None, :]).astype(BF16)
    return pl.pallas_call(
        kern,
        grid=(nb, m // tt),
        in_specs=[pl.BlockSpec((tt, LANES), lambda b, t: (t, b)),
                  pl.BlockSpec((SCAN_UNIT, SCAN_UNIT), lambda b, t: (0, 0)),
                  pl.BlockSpec((SCAN_UNIT, SCAN_UNIT), lambda b, t: (0, 0)),
                  pl.BlockSpec((1, LANES, 2 * STATE_LANES), lambda b, t: (b, 0, 0)),
                  pl.BlockSpec((1, 2 * STATE_LANES, LANES), lambda b, t: (b, 0, 0)),
                  pl.BlockSpec((1,) + cp.shape[1:], lambda b, t: (b, 0, 0)),
                  pl.BlockSpec((1, SUBLANES, 2 * STATE_LANES), lambda b, t: (b, 0, 0)),
                  pl.BlockSpec((1, SCAN_UNIT, 2 * STATE_LANES), lambda b, t: (b, 0, 0)),
                  pl.BlockSpec((1, 1, LANES), lambda b, t: (b, 0, 0))],
        out_specs=pl.BlockSpec((tt, LANES), lambda b, t: (t, b)),
        out_shape=jax.ShapeDtypeStruct((m, nb * LANES), BF16),
        scratch_shapes=[pltpu.VMEM((tt // SCAN_UNIT, SCAN_UNIT, 2 * STATE_LANES), F32),
                        pltpu.VMEM((SUBLANES, 2 * STATE_LANES), F32)],
        compiler_params=_cparams("arbitrary", "arbitrary"),
        name="s5_scan",
    )(proj, perm, perm.T, wb, wc, cp, cps, cp[:, :SCAN_UNIT].astype(BF16), d)


def _ssm_params(lam_re, lam_im, log_step, b_re, b_im, c_re, c_im, d):
    g, p = lam_re.shape
    h = b_re.shape[-1]
    assert (h, p) == (SSM_GROUP, SSM_STATE) and g % GROUPS_PER_BLOCK == 0
    nb = g // GROUPS_PER_BLOCK
    lr = lam_re.astype(F32)
    li = lam_im.astype(F32)
    step = jnp.exp(log_step.astype(F32))[:, None]
    mag = jnp.exp(lr * step)
    ang = li * step
    abar_re, abar_im = mag * jnp.cos(ang), mag * jnp.sin(ang)
    den = lr * lr + li * li
    nr, ni = abar_re - 1.0, abar_im
    coef_re = (nr * lr + ni * li) / den
    coef_im = (ni * lr - nr * li) / den
    br_f, bi_f = b_re.astype(F32), b_im.astype(F32)
    bbar_re = coef_re[..., None] * br_f - coef_im[..., None] * bi_f
    bbar_im = coef_re[..., None] * bi_f + coef_im[..., None] * br_f
    eye = jnp.eye(GROUPS_PER_BLOCK, dtype=F32)

    def in_block(x):
        x = x.reshape(nb, GROUPS_PER_BLOCK, p, h).transpose(0, 1, 3, 2)
        x = jnp.einsum('lghp,gk->lghkp', x, eye)
        return x.reshape(nb, LANES, STATE_LANES)

    def out_block(x):
        x = x.reshape(nb, GROUPS_PER_BLOCK, h, p).transpose(0, 1, 3, 2)
        x = jnp.einsum('lgph,gk->lgpkh', x, eye)
        return x.reshape(nb, STATE_LANES, LANES)

    wb = jnp.concatenate([in_block(bbar_re), in_block(bbar_im)], axis=-1).astype(BF16)
    wc = jnp.concatenate([out_block(c_re.astype(F32)), out_block(-c_im.astype(F32))],
                         axis=1).astype(BF16)

    a_re = abar_re.reshape(nb, STATE_LANES)
    a_im = abar_im.reshape(nb, STATE_LANES)
    pows = [(a_re, a_im)]
    for _ in range(SCAN_SUB - 1):
        pows.append(_cmul(a_re, a_im, *pows[-1]))
    sub = pows[-1]
    one = (jnp.ones_like(a_re), jnp.zeros_like(a_im))
    per_sublane = [one]
    for _ in range(SUBLANES - 1):
        per_sublane.append(_cmul(*sub, *per_sublane[-1]))
    sub2 = _cmul(*sub, *sub)
    sub4 = _cmul(*sub2, *sub2)
    sub8 = _cmul(*sub4, *sub4)

    def table(pairs):
        return jnp.stack([jnp.concatenate(c, axis=-1) for c in pairs], axis=1)

    cp = jnp.repeat(table(pows + [sub, sub2, sub4, sub8]), SUBLANES, axis=1)
    return wb, wc, cp, table(per_sublane), d.astype(F32).reshape(nb, 1, LANES)


def _mix_kernel(z_ref, gw_ref, gb_ref, ng_s_ref, pb_ref, pc_ref, pv_ref, hc_ref, hv_ref,
                cw_ref, ng_c_ref, o_ref, lin_ref, yc_ref, sq_ref,
                *, tm, width, tiles_per_seq, n_chain):
    z = z_ref[...]
    lin_ref[...] = jnp.dot(z, gw_ref[...], preferred_element_type=F32)

    n_cols = width // LANES
    cols_per_chain = n_cols // n_chain
    rc = BF16_ROWS
    first = pl.program_id(0) % tiles_per_seq == 0
    above = []
    for col in range(n_cols):
        lanes = pl.ds(col * LANES, LANES)
        halo = hc_ref[:, lanes].astype(F32) * hv_ref[:, lanes].astype(F32)
        above.append(jnp.where(first, 0.0, halo))
    zeros = [None] * n_chain
    for r0 in range(0, tm, rc):
        rows = pl.ds(r0, rc)
        sq = [None] * n_chain
        for k in range(cols_per_chain):
            for c in range(n_chain):
                col = c * cols_per_chain + k
                lanes = pl.ds(col * LANES, LANES)
                cv = _after(pc_ref[rows, lanes].astype(F32), zeros[c]) \
                    * pv_ref[rows, lanes].astype(F32)
                ext = jnp.concatenate([above[col], cv], axis=0)
                conv = (cw_ref[0:1, lanes] * ext[SUBLANES - 2:SUBLANES - 2 + rc]
                        + cw_ref[1:2, lanes] * ext[SUBLANES - 1:SUBLANES - 1 + rc]
                        + cw_ref[2:3, lanes] * cv)
                yc = pb_ref[rows, lanes].astype(F32) * conv
                yc_ref[rows, lanes] = yc
                sq[c] = yc * yc if sq[c] is None else sq[c] + yc * yc
                above[col] = cv[rc - SUBLANES:]
                zeros[c] = _chain_zero(yc)
        total = sq[0]
        for part in sq[1:]:
            total = total + part
        sq_ref[rows, :] = total

    y = z.astype(F32) * jax.nn.sigmoid(lin_ref[...] + gb_ref[...])
    o_ref[:, pl.ds(0, width)] = (y * _rms_scale(y) * ng_s_ref[...]).astype(o_ref.dtype)

    ms = jnp.sum(sq_ref[...], axis=-1, keepdims=True) * (1.0 / width)
    o_ref[:, pl.ds(width, width)] = (yc_ref[...] * lax.rsqrt(ms + EPS)
                                     * ng_c_ref[...]).astype(o_ref.dtype)


def _mix(z, proj, glu_w, glu_b, ng_s, conv_w, ng_c, *, seq_len):
    m, width = z.shape
    tm = MIX_ROWS
    assert proj.shape[1] == 4 * width and seq_len % tm == 0
    assert (width // LANES) % MIX_CHAINS == 0
    halo_blocks = tm // SUBLANES
    kern = functools.partial(_mix_kernel, tm=tm, width=width, tiles_per_seq=seq_len // tm,
                             n_chain=MIX_CHAINS)
    row = lambda a: a.reshape(1, width).astype(F32)

    def halo_map(col):
        return lambda i: (jnp.maximum(i * halo_blocks - 1, 0), col)

    return pl.pallas_call(
        kern,
        grid=(m // tm,),
        in_specs=[pl.BlockSpec((tm, width), lambda i: (i, 0)),
                  pl.BlockSpec((width, width), lambda i: (0, 0)),
                  pl.BlockSpec((1, width), lambda i: (0, 0)),
                  pl.BlockSpec((1, width), lambda i: (0, 0)),
                  pl.BlockSpec((tm, width), lambda i: (i, 1)),
                  pl.BlockSpec((tm, width), lambda i: (i, 2)),
                  pl.BlockSpec((tm, width), lambda i: (i, 3)),
                  pl.BlockSpec((SUBLANES, width), halo_map(2)),
                  pl.BlockSpec((SUBLANES, width), halo_map(3)),
                  pl.BlockSpec((3, width), lambda i: (0, 0)),
                  pl.BlockSpec((1, width), lambda i: (0, 0))],
        out_specs=pl.BlockSpec((tm, 2 * width), lambda i: (i, 0)),
        out_shape=jax.ShapeDtypeStruct((m, 2 * width), BF16),
        scratch_shapes=[pltpu.VMEM((tm, width), F32), pltpu.VMEM((tm, width), F32),
                        pltpu.VMEM((tm, LANES), F32)],
        compiler_params=_cparams("arbitrary"),
        name="mixer_tail",
    )(z, glu_w, row(glu_b), row(ng_s), proj, proj, proj, proj, proj,
      conv_w.astype(F32), row(ng_c))


def _xattn_kernel(hq_ref, ssq_ref, wq_ref, k_ref, v_ref, o_ref, *, scale):
    q = jnp.dot(hq_ref[...], wq_ref[...], preferred_element_type=F32)
    s = lax.dot_general(q.astype(BF16), k_ref[...], (((1,), (1,)), ((), ())),
                        preferred_element_type=F32)
    rs = _row_scale(ssq_ref[...], hq_ref.shape[1]) * scale
    s = s * jnp.concatenate([rs] * (s.shape[1] // LANES), axis=1)
    s = s - jnp.max(s, axis=-1, keepdims=True)
    e = jnp.exp(s)
    p = e / jnp.sum(e, axis=-1, keepdims=True)
    o_ref[...] = jnp.dot(p.astype(BF16), v_ref[...],
                         preferred_element_type=F32).astype(o_ref.dtype)


def _xattn(hq, ssq, wq, k, v, *, seq_len, n_mem):
    m, d = hq.shape
    hd = d // XATTN_HEADS
    tm = XATTN_ROWS
    assert seq_len % tm == 0 and n_mem % LANES == 0
    tiles_per_seq = seq_len // tm
    kern = functools.partial(_xattn_kernel, scale=float(hd) ** -0.5)
    return pl.pallas_call(
        kern,
        grid=(m // tm, XATTN_HEADS),
        in_specs=[pl.BlockSpec((tm, d), lambda i, h: (i, 0)),
                  pl.BlockSpec((tm, LANES), lambda i, h: (i, 0)),
                  pl.BlockSpec((d, hd), lambda i, h: (0, h)),
                  pl.BlockSpec((n_mem, hd), lambda i, h: (i // tiles_per_seq, h)),
                  pl.BlockSpec((n_mem, hd), lambda i, h: (i // tiles_per_seq, h))],
        out_specs=pl.BlockSpec((tm, hd), lambda i, h: (i, h)),
        out_shape=jax.ShapeDtypeStruct((m, d), BF16),
        compiler_params=_cparams("arbitrary", "arbitrary"),
        name="xattn",
    )(hq, ssq, wq, k, v)


def _ffn_up_kernel(h_ref, ssq_ref, wa_ref, wg_ref, cw_ref, cb_ref, o_ref,
                   wab_ref, wgb_ref, a0_ref, g0_ref, a1_ref, g1_ref, stage_ref, halo_ref,
                   *, tm, n_i, n_steps, tiles_per_seq, n_phase):
    t = pl.program_id(0)
    i = t % n_i

    @pl.when(jnp.logical_and(i == 0, t < n_steps - 1))
    def _():
        wab_ref[...] = wa_ref[...].astype(BF16)
        wgb_ref[...] = wg_ref[...].astype(BF16)

    mc = tm // n_phase
    rc = BF16_ROWS
    n_cols = o_ref.shape[1] // LANES
    tail_seq_start = (jnp.maximum(t - 1, 0) % n_i) % tiles_per_seq == 0

    def tail_piece(a_prev, g_prev, r0, col, above, zero):
        lanes = pl.ds(col * LANES, LANES)
        rows = pl.ds(r0, rc)
        rs = _row_scale(ssq_ref[rows, :], h_ref.shape[1])
        a = _after(a_prev[rows, lanes], zero) * rs
        ext = jnp.concatenate([above, a], axis=0)
        conv = (cw_ref[0:1, lanes] * ext[SUBLANES - 2:SUBLANES - 2 + rc]
                + cw_ref[1:2, lanes] * ext[SUBLANES - 1:SUBLANES - 1 + rc]
                + cw_ref[2:3, lanes] * a) + cb_ref[:, lanes]
        out = jax.nn.silu(conv) * (g_prev[rows, lanes] * rs)
        stage_ref[rows, lanes] = out.astype(stage_ref.dtype)
        return a[rc - SUBLANES:], _chain_zero(out)

    def step(cur, prev, dots, tail):
        kc = h_ref.shape[1] // n_phase
        a_acc = g_acc = None
        zeros = [None] * n_cols
        if tail:
            above = [jnp.where(tail_seq_start, 0.0, halo_ref[col]) for col in range(n_cols)]
        for p in range(n_phase):
            if dots:
                hb = h_ref[:, pl.ds(p * kc, kc)]
                pa = jnp.dot(hb, wab_ref[pl.ds(p * kc, kc), :], preferred_element_type=F32)
                pg = jnp.dot(hb, wgb_ref[pl.ds(p * kc, kc), :], preferred_element_type=F32)
                a_acc = pa if a_acc is None else a_acc + pa
                g_acc = pg if g_acc is None else g_acc + pg
            if tail:
                for r0 in range(p * mc, (p + 1) * mc, rc):
                    for col in range(n_cols):
                        above[col], zeros[col] = tail_piece(prev[0], prev[1], r0, col,
                                                            above[col], zeros[col])
        if tail:
            for col in range(n_cols):
                halo_ref[col] = above[col]
            o_ref[...] = stage_ref[...]
        if dots:
            cur[0][...] = a_acc
            cur[1][...] = g_acc

    slots = ((a0_ref, g0_ref), (a1_ref, g1_ref))
    middle = jnp.logical_and(t > 0, t < n_steps - 1)

    @pl.when(t == 0)
    def _():
        halo_ref[...] = jnp.zeros_like(halo_ref)
        step(slots[0], None, dots=True, tail=False)

    for parity in (0, 1):
        @pl.when(jnp.logical_and(middle, t % 2 == parity))
        def _():
            step(slots[parity], slots[1 - parity], dots=True, tail=True)

    @pl.when(t == n_steps - 1)
    def _():
        step(None, slots[(n_steps - 2) % 2], dots=False, tail=True)


def _ffn_up(h, ssq, w_up, conv_w, conv_b, *, seq_len):
    m, d = h.shape
    d_ff = conv_w.shape[-1]
    tm, tf = FFN_UP_TILE
    assert seq_len % tm == 0 and d_ff % tf == 0 and w_up.shape == (d, 2 * d_ff)
    assert d % OVERLAP_PHASES == 0 and tm % (OVERLAP_PHASES * BF16_ROWS) == 0
    nf = d_ff // tf
    n_i = m // tm
    n_steps = nf * n_i + 1
    kern = functools.partial(_ffn_up_kernel, tm=tm, n_i=n_i, n_steps=n_steps,
                             n_phase=OVERLAP_PHASES, tiles_per_seq=seq_len // tm)

    def tile_j(t):
        return jnp.minimum(t, n_steps - 2) // n_i

    def tail_i(t):
        return jnp.maximum(t - 1, 0) % n_i

    def tail_j(t):
        return jnp.maximum(t - 1, 0) // n_i

    return pl.pallas_call(
        kern,
        grid=(n_steps,),
        in_specs=[pl.BlockSpec((tm, d), lambda t: (jnp.minimum(t, n_steps - 2) % n_i, 0)),
                  pl.BlockSpec((tm, LANES), lambda t: (tail_i(t), 0)),
                  pl.BlockSpec((d, tf), lambda t: (0, tile_j(t))),
                  pl.BlockSpec((d, tf), lambda t: (0, tile_j(t) + nf)),
                  pl.BlockSpec((3, tf), lambda t: (0, tail_j(t))),
                  pl.BlockSpec((1, tf), lambda t: (0, tail_j(t)))],
        out_specs=pl.BlockSpec((tm, tf), lambda t: (tail_i(t), tail_j(t))),
        out_shape=jax.ShapeDtypeStruct((m, d_ff), BF16),
        scratch_shapes=[pltpu.VMEM((d, tf), BF16), pltpu.VMEM((d, tf), BF16),
                        pltpu.VMEM((tm, tf), F32), pltpu.VMEM((tm, tf), F32),
                        pltpu.VMEM((tm, tf), F32), pltpu.VMEM((tm, tf), F32),
                        pltpu.VMEM((tm, tf), BF16),
                        pltpu.VMEM((tf // LANES, SUBLANES, LANES), F32)],
        compiler_params=_cparams("arbitrary"),
        name="ffn_up",
    )(h, ssq, w_up, w_up, conv_w.astype(F32), conv_b.reshape(1, d_ff).astype(F32))


def kernel(x, mem, norm_mix_g, w_in, ssm_lambda_re, ssm_lambda_im, ssm_log_step, ssm_b_re, ssm_b_im, ssm_c_re, ssm_c_im, ssm_d, ssm_glu_w, ssm_glu_b, conv_w, out_norm_ssm_g, out_norm_conv_g, w_out, norm_xattn_g, norm_mem_g, xattn_wq, xattn_wk, xattn_wv, xattn_wo, norm_ffn_g, ffn_w_up, ffn_conv_w, ffn_conv_b, ffn_w_down, norm_final_g):
    bsz, seq_len, d_model = x.shape
    n_mem = mem.shape[1]
    depth = w_in.shape[0]
    m = bsz * seq_len
    xs = x.reshape(m, d_model)
    mems = mem.reshape(bsz * n_mem, d_model)
    for l in range(depth):
        h = _rmsnorm(xs, norm_mix_g[l], BF16)
        proj = _matmul(h, w_in[l].astype(BF16), tile=PROJ_IN_TILE, out_dtype=BF16, name="proj_in")
        ssm_tables = _ssm_params(ssm_lambda_re[l], ssm_lambda_im[l], ssm_log_step[l], ssm_b_re[l],
                                 ssm_b_im[l], ssm_c_re[l], ssm_c_im[l], ssm_d[l])
        z = _ssm(proj, *ssm_tables, seq_len=seq_len)
        mixed = _mix(z, proj, ssm_glu_w[l].astype(BF16), ssm_glu_b[l], out_norm_ssm_g[l],
                     conv_w[l], out_norm_conv_g[l], seq_len=seq_len)
        xs, hq, ssq = _matmul_res_norm(mixed, w_out[l].astype(BF16), xs, norm_xattn_g[l],
                                       name="proj_out")
        hm = _rmsnorm(mems, norm_mem_g[l], BF16)
        k = _matmul(hm, xattn_wk[l], tile=KV_TILE, out_dtype=BF16, name="proj_k")
        v = _matmul(hm, xattn_wv[l], tile=KV_TILE, out_dtype=BF16, name="proj_v")
        o = _xattn(hq, ssq, xattn_wq[l].astype(BF16), k, v, seq_len=seq_len, n_mem=n_mem)
        xs, h, ssq = _matmul_res_norm(o, xattn_wo[l].astype(BF16), xs, norm_ffn_g[l],
                                      name="proj_o")
        act = _ffn_up(h, ssq, ffn_w_up[l], ffn_conv_w[l], ffn_conv_b[l], seq_len=seq_len)
        xs = _matmul(act, ffn_w_down[l].astype(BF16), tile=FFN_DOWN_TILE, out_dtype=F32,
                     residual=xs, name="ffn_down")
    out = _rmsnorm(xs, norm_final_g, x.dtype)
    return out.reshape(bsz, seq_len, d_model)
```
